```python
import jax, jax.numpy as jnp
from jax import lax
import numpy as np

D_MODEL = 1024
BATCH = 8
SEQ = 4096
DEPTH = 2

GRID_W = 64
CTX_LEN = 256
N_MIXERS = 2
EPS = 1e-6
CHUNK = 128
A_WIDTH = 2 * D_MODEL
A_GROUPS = 8
A_GROUP_DIM = A_WIDTH // A_GROUPS
HEAD_DIM = 64
N_HEADS = D_MODEL // HEAD_DIM
N_KV_HEADS = 4
Q_PER_KV = N_HEADS // N_KV_HEADS
WINDOW = 128
BLOCK = 128
ROPE_THETA = 10000.0
N_FREQ = HEAD_DIM // 4
QKV_DIM = (N_HEADS + 2 * N_KV_HEADS) * HEAD_DIM
FFN_DIM = 2816
CONV_W = 3
N_LAYERS_A = (DEPTH + 1) // 2
N_LAYERS_B = DEPTH // 2

kernel_name = "hybrid_gmlp_swa_convglu_dit"


def rms_norm(x, g):
    xf = x.astype(jnp.float32)
    y = xf * lax.rsqrt(jnp.mean(xf * xf, axis=-1, keepdims=True) + EPS)
    return (y * g.astype(jnp.float32)).astype(x.dtype)


def layer_norm(x, g, b):
    xf = x.astype(jnp.float32)
    mu = jnp.mean(xf, axis=-1, keepdims=True)
    var = jnp.mean(jnp.square(xf - mu), axis=-1, keepdims=True)
    y = (xf - mu) * lax.rsqrt(var + EPS)
    return (y * g.astype(jnp.float32) + b.astype(jnp.float32)).astype(x.dtype)


def modulate(h, shift, scale):
    return h * (1 + scale) + shift


def chunk_mlp(h, w_in, ln_g, ln_b, w_s, b_s, w_out):
    bsz, n_tok, _ = h.shape
    z = jax.nn.gelu(h @ w_in)
    u, v = jnp.split(z, 2, axis=-1)
    v = layer_norm(v, ln_g, ln_b)
    v = v.reshape(bsz, n_tok // CHUNK, CHUNK, A_GROUPS, A_GROUP_DIM)
    s = jnp.einsum('gpq,bnqgc->bnpgc', w_s, v) + b_s.T[None, None, :, :, None]
    return (u * s.reshape(bsz, n_tok, A_WIDTH)) @ w_out


def axial_rope_tables(n_tok):
    ROWS = n_tok // GRID_W
    row = jnp.repeat(jnp.arange(ROWS), GRID_W).astype(jnp.float32)
    col = jnp.tile(jnp.arange(GRID_W), ROWS).astype(jnp.float32)
    inv_freq = ROPE_THETA ** (-(jnp.arange(N_FREQ, dtype=jnp.float32) / N_FREQ))
    ang = jnp.concatenate([row[:, None] * inv_freq, col[:, None] * inv_freq], axis=-1)
    return jnp.cos(ang), jnp.sin(ang)


def apply_rope(x, cos, sin):
    shape = (1, cos.shape[0]) + (1,) * (x.ndim - 3) + (cos.shape[-1],)
    c = cos.reshape(shape).astype(x.dtype)
    s = sin.reshape(shape).astype(x.dtype)
    x1, x2 = jnp.split(x, 2, axis=-1)
    return jnp.concatenate([x1 * c - x2 * s, x2 * c + x1 * s], axis=-1)


def qkv_heads(h, w_qkv, b_qkv):
    bsz, n_tok, _ = h.shape
    qkv = h @ w_qkv + b_qkv
    q, k, v = jnp.split(qkv, [N_HEADS * HEAD_DIM, (N_HEADS + N_KV_HEADS) * HEAD_DIM], axis=-1)
    q = q.reshape(bsz, n_tok, N_KV_HEADS, Q_PER_KV, HEAD_DIM)
    k = k.reshape(bsz, n_tok, N_KV_HEADS, HEAD_DIM)
    v = v.reshape(bsz, n_tok, N_KV_HEADS, HEAD_DIM)
    return q, k, v


def softmax_with_sink(logits, sink):
    sink_col = jnp.broadcast_to(sink.astype(jnp.float32)[None, :, :, None, None], logits.shape[:-1] + (1,))
    p = jax.nn.softmax(jnp.concatenate([logits, sink_col], axis=-1), axis=-1)
    return p[..., :-1]


def window_attention(h, hc, w_qkv, b_qkv, sink, w_o, cos, sin, with_ctx_out):
    bsz, n_tok, _ = h.shape
    scale = HEAD_DIM ** -0.5
    sink = sink.reshape(N_KV_HEADS, Q_PER_KV)
    q, k, v = qkv_heads(h, w_qkv, b_qkv)
    qc, kc, vc = qkv_heads(hc, w_qkv, b_qkv)
    q = apply_rope(q, cos, sin)
    k = apply_rope(k, cos, sin)

    n_blk = n_tok // BLOCK
    band = 3 * BLOCK
    pad = ((0, 0), (BLOCK, BLOCK), (0, 0), (0, 0))
    kp = jnp.pad(k, pad)
    vp = jnp.pad(v, pad)
    qb = jnp.moveaxis(q.reshape(bsz, n_blk, BLOCK, N_KV_HEADS, Q_PER_KV, HEAD_DIM), 1, 0)
    rel = jnp.arange(BLOCK)[:, None] - jnp.arange(band)[None, :] + BLOCK
    in_window = jnp.abs(rel) <= WINDOW

    def one_block(args):
        j, qj = args
        kj = lax.dynamic_slice_in_dim(kp, j * BLOCK, band, axis=1)
        vj = lax.dynamic_slice_in_dim(vp, j * BLOCK, band, axis=1)
        kpos = j * BLOCK - BLOCK + jnp.arange(band)
        valid = in_window & ((kpos >= 0) & (kpos < n_tok))[None, :]
        s_lat = jnp.einsum('bqhgd,bkhd->bhgqk', qj, kj).astype(jnp.float32) * scale
        s_lat = jnp.where(valid, s_lat, -jnp.inf)
        s_ctx = jnp.einsum('bqhgd,bkhd->bhgqk', qj, kc).astype(jnp.float32) * scale
        p = softmax_with_sink(jnp.concatenate([s_lat, s_ctx], axis=-1), sink)
        p_lat = p[..., :band].astype(vj.dtype)
        p_ctx = p[..., band:].astype(vc.dtype)
        return (jnp.einsum('bhgqk,bkhd->bqhgd', p_lat, vj)
                + jnp.einsum('bhgqk,bkhd->bqhgd', p_ctx, vc))

    out = lax.map(one_block, (jnp.arange(n_blk), qb))
    y = jnp.moveaxis(out, 0, 1).reshape(bsz, n_tok, N_HEADS * HEAD_DIM) @ w_o

    if not with_ctx_out:
        return y, None
    s_cc = jnp.einsum('bqhgd,bkhd->bhgqk', qc, kc).astype(jnp.float32) * scale
    p_cc = softmax_with_sink(s_cc, sink).astype(vc.dtype)
    yc = jnp.einsum('bhgqk,bkhd->bqhgd', p_cc, vc).reshape(bsz, hc.shape[1], N_HEADS * HEAD_DIM) @ w_o
    return y, yc


def conv_ffn(h, w_up, conv_w, conv_b, w_down):
    a, b = jnp.split(h @ w_up, 2, axis=-1)
    ap = jnp.pad(a, ((0, 0), (1, 1), (0, 0)))
    a = ap[:, :-2] * conv_w[0] + ap[:, 1:-1] * conv_w[1] + ap[:, 2:] * conv_w[2] + conv_b
    return (jax.nn.gelu(a) * b) @ w_down


def setup_inputs(seed: int = 0) -> dict:
    key = jax.random.key(seed)
    ks = jax.random.split(key, 24)
    n = lambda k, shape: jax.random.normal(k, shape, jnp.float32)
    D, E, F = D_MODEL, A_WIDTH, FFN_DIM
    return {
        "x": n(ks[0], (BATCH, SEQ, D)),
        "c": n(ks[1], (BATCH, D)),
        "ctx": n(ks[2], (BATCH, CTX_LEN, D)),
        "c_ctx": n(ks[3], (D,)),
        "ada_w": n(ks[4], (DEPTH, D, 6 * D)) * (0.5 * D ** -0.5),
        "ada_b": n(ks[5], (DEPTH, 6 * D)) * 0.01,
        "norm_g": 1.0 + 0.05 * n(ks[6], (DEPTH, 4, D)),
        "a_w_in": n(ks[7], (N_LAYERS_A, D, 2 * E)) * D ** -0.5,
        "a_ln_g": 1.0 + 0.05 * n(ks[8], (N_LAYERS_A, E)),
        "a_ln_b": 0.02 * n(ks[9], (N_LAYERS_A, E)),
        "a_w_s": n(ks[10], (N_LAYERS_A, A_GROUPS, CHUNK, CHUNK)) * CHUNK ** -0.5,
        "a_b_s": 1.0 + 0.05 * n(ks[11], (N_LAYERS_A, A_GROUPS, CHUNK)),
        "a_w_out": n(ks[12], (N_LAYERS_A, E, D)) * E ** -0.5,
        "b_w_qkv": n(ks[13], (N_LAYERS_B, D, QKV_DIM)) * D ** -0.5,
        "b_b_qkv": 0.02 * n(ks[14], (N_LAYERS_B, QKV_DIM)),
        "b_sink": n(ks[15], (N_LAYERS_B, N_HEADS)),
        "b_w_o": n(ks[16], (N_LAYERS_B, N_HEADS * HEAD_DIM, D)) * (N_HEADS * HEAD_DIM) ** -0.5,
        "f_w_up": n(ks[17], (DEPTH, D, 2 * F)) * D ** -0.5,
        "f_conv_w": n(ks[18], (DEPTH, CONV_W, F)) * CONV_W ** -0.5,
        "f_conv_b": 0.02 * n(ks[19], (DEPTH, F)),
        "f_w_down": n(ks[20], (DEPTH, F, D)) * F ** -0.5,
    }


def reference(x, c, ctx, c_ctx, ada_w, ada_b, norm_g, a_w_in, a_ln_g, a_ln_b, a_w_s, a_b_s,
              a_w_out, b_w_qkv, b_b_qkv, b_sink, b_w_o, f_w_up, f_conv_w, f_conv_b, f_w_down):
    n_tok = x.shape[1]
    cos, sin = axial_rope_tables(n_tok)
    silu_c = jax.nn.silu(c)
    silu_cc = jax.nn.silu(c_ctx)
    for i in range(DEPTH):
        last = i == DEPTH - 1
        mod_lat = (silu_c @ ada_w[i] + ada_b[i])[:, None, :]
        mod_ctx = (silu_cc @ ada_w[i] + ada_b[i])[None, None, :]
        sh1, sc1, g1, sh2, sc2, g2 = jnp.split(mod_lat, 6, axis=-1)
        csh1, csc1, cg1, csh2, csc2, cg2 = jnp.split(mod_ctx, 6, axis=-1)

        h = modulate(rms_norm(x, norm_g[i, 0]), sh1, sc1)
        hc = modulate(rms_norm(ctx, norm_g[i, 0]), csh1, csc1)
        j = i // N_MIXERS
        if i % N_MIXERS == 0:
            y = chunk_mlp(h, a_w_in[j], a_ln_g[j], a_ln_b[j], a_w_s[j], a_b_s[j], a_w_out[j])
            yc = None if last else chunk_mlp(hc, a_w_in[j], a_ln_g[j], a_ln_b[j], a_w_s[j], a_b_s[j], a_w_out[j])
        else:
            y, yc = window_attention(h, hc, b_w_qkv[j], b_b_qkv[j], b_sink[j], b_w_o[j], cos, sin, not last)
        x = x + g1 * rms_norm(y, norm_g[i, 1])
        if not last:
            ctx = ctx + cg1 * rms_norm(yc, norm_g[i, 1])

        h = modulate(rms_norm(x, norm_g[i, 2]), sh2, sc2)
        x = x + g2 * rms_norm(conv_ffn(h, f_w_up[i], f_conv_w[i], f_conv_b[i], f_w_down[i]), norm_g[i, 3])
        if not last:
            hc = modulate(rms_norm(ctx, norm_g[i, 2]), csh2, csc2)
            ctx = ctx + cg2 * rms_norm(conv_ffn(hc, f_w_up[i], f_conv_w[i], f_conv_b[i], f_w_down[i]), norm_g[i, 3])
    return x
```

```python
import functools
import math

import jax
import jax.numpy as jnp
from jax import lax
from jax.experimental import pallas as pl
from jax.experimental.pallas import tpu as pltpu

EPS = 1e-6
GRID_W = 64
CHUNK = 128
A_GROUPS = 8
HEAD_DIM = 64
N_HEADS = 16
N_KV_HEADS = 4
Q_PER_KV = N_HEADS // N_KV_HEADS
BLOCK = 128
ROPE_THETA = 10000.0
HALO = 16
VMEM_LIMIT = 56 * 1024 * 1024

F32 = jnp.float32
BF16 = jnp.bfloat16


def _rms(x, g):
    return x * lax.rsqrt(jnp.mean(x * x, axis=-1, keepdims=True) + EPS) * g


def _gelu(x):
    c = math.sqrt(2.0 / math.pi)
    return x * (0.5 * (1.0 + jnp.tanh(c * (x + 0.044715 * (x * x * x)))))


def _dot(a, b):
    return jnp.dot(a, b, preferred_element_type=F32)


def _params(n_grid):
    return pltpu.CompilerParams(dimension_semantics=("parallel",) * n_grid,
                                vmem_limit_bytes=VMEM_LIMIT)


def _const_spec(shape):
    nd = len(shape)
    return pl.BlockSpec(shape, lambda *_: (0,) * nd, pipeline_mode=pl.Buffered(1))


def _ada_kernel(s_ref, w_ref, b_ref, o_ref):
    s = s_ref[...]
    s = s * jax.nn.sigmoid(s)
    o_ref[0] = _dot(s.astype(BF16), w_ref[0].astype(BF16)) + b_ref[0]


def _ada(cond, ada_w, ada_b, tn=1536):
    depth, d, n = ada_w.shape
    r = cond.shape[0]
    return pl.pallas_call(
        _ada_kernel,
        grid=(depth, n // tn),
        in_specs=[pl.BlockSpec((r, d), lambda i, j: (0, 0)),
                  pl.BlockSpec((1, d, tn), lambda i, j: (i, 0, j)),
                  pl.BlockSpec((1, 1, tn), lambda i, j: (i, 0, j))],
        out_specs=pl.BlockSpec((1, r, tn), lambda i, j: (i, 0, j)),
        out_shape=jax.ShapeDtypeStruct((depth, r, n), F32),
        compiler_params=_params(2),
        name="ada_mod",
    )(cond, ada_w, ada_b.reshape(depth, 1, n))


def _gmlp_kernel(x_ref, mod_ref, ng_ref, w_in_ref, lng_ref, lnb_ref, ws_ref, bs_ref, w_out_ref,
                 o_ref, t_ref, *, tile, width):
    x = x_ref[0]
    mod = mod_ref[0]
    sh, sc, gate = mod[0:1], mod[1:2], mod[2:3]
    h = _rms(x, ng_ref[0:1]) * (1.0 + sc) + sh
    z = _gelu(_dot(h.astype(BF16), w_in_ref[...]))
    u = z[:, :width]
    v = z[:, width:]
    mu = jnp.mean(v, axis=-1, keepdims=True)
    vc = v - mu
    var = jnp.mean(vc * vc, axis=-1, keepdims=True)
    vn = (vc * lax.rsqrt(var + EPS) * lng_ref[...] + lnb_ref[...]).astype(BF16)
    gw = width // A_GROUPS
    for c in range(tile // CHUNK):
        rows = slice(c * CHUNK, (c + 1) * CHUNK)
        for g in range(A_GROUPS):
            cols = slice(g * gw, (g + 1) * gw)
            s = _dot(ws_ref[g], vn[rows, cols]) + bs_ref[g]
            t_ref[rows, cols] = (u[rows, cols] * s).astype(BF16)
    y = _dot(t_ref[...], w_out_ref[...])
    o_ref[0] = x + gate * _rms(y, ng_ref[1:2])


def _gmlp(x, mod, ng, w_in, ln_g, ln_b, w_s, b_s, w_out, tile=256):
    bsz, n_tok, d = x.shape
    width = w_out.shape[0]
    gw = width // A_GROUPS
    bs_full = jnp.broadcast_to(b_s[:, :, None], (A_GROUPS, CHUNK, gw))
    kern = functools.partial(_gmlp_kernel, tile=tile, width=width)
    return pl.pallas_call(
        kern,
        grid=(bsz, n_tok // tile),
        in_specs=[pl.BlockSpec((1, tile, d), lambda b, i: (b, i, 0)),
                  pl.BlockSpec((1, 6, d), lambda b, i: (b, 0, 0)),
                  _const_spec(ng.shape),
                  _const_spec(w_in.shape),
                  _const_spec((1, width)),
                  _const_spec((1, width)),
                  _const_spec(w_s.shape),
                  _const_spec(bs_full.shape),
                  _const_spec(w_out.shape)],
        out_specs=pl.BlockSpec((1, tile, d), lambda b, i: (b, i, 0)),
        out_shape=jax.ShapeDtypeStruct(x.shape, F32),
        scratch_shapes=[pltpu.VMEM((tile, width), BF16)],
        compiler_params=_params(2),
        name="gmlp_mixer",
    )(x, mod, ng, w_in, ln_g.reshape(1, width), ln_b.reshape(1, width), w_s, bs_full, w_out)


def _ffn_kernel(x_ref, xp_ref, xn_ref, mod_ref, ng_ref, w_up_ref, cw_ref, cb_ref, w_dn_ref,
                o_ref, h_ref, a_ref, *, tile, n_tiles, ffn):
    i = pl.program_id(1)
    mod = mod_ref[0]
    sh, sc, gate = mod[3:4], mod[4:5], mod[5:6]
    g_pre = ng_ref[2:3]

    def pre(xx):
        return _rms(xx, g_pre) * (1.0 + sc) + sh

    x = x_ref[0]
    has_prev = (i > 0).astype(F32)
    has_next = (i < n_tiles - 1).astype(F32)
    h_ref[0:HALO] = (pre(xp_ref[0]) * has_prev).astype(BF16)
    h_ref[HALO:HALO + tile] = pre(x).astype(BF16)
    h_ref[HALO + tile:] = (pre(xn_ref[0]) * has_next).astype(BF16)
    a_ref[...] = _dot(h_ref[...], w_up_ref[:, :ffn])
    b = _dot(h_ref[HALO:HALO + tile], w_up_ref[:, ffn:])
    a = (a_ref[HALO - 1:HALO - 1 + tile] * cw_ref[0:1]
         + a_ref[HALO:HALO + tile] * cw_ref[1:2]
         + a_ref[HALO + 1:HALO + 1 + tile] * cw_ref[2:3]
         + cb_ref[...])
    y = _dot((_gelu(a) * b).astype(BF16), w_dn_ref[...])
    o_ref[0] = x + gate * _rms(y, ng_ref[3:4])


def _ffn(x, mod, ng, w_up, conv_w, conv_b, w_down, tile=256):
    bsz, n_tok, d = x.shape
    ffn = w_down.shape[0]
    n_tiles = n_tok // tile
    hb = tile // HALO
    n_hb = n_tok // HALO
    kern = functools.partial(_ffn_kernel, tile=tile, n_tiles=n_tiles, ffn=ffn)
    return pl.pallas_call(
        kern,
        grid=(bsz, n_tiles),
        in_specs=[pl.BlockSpec((1, tile, d), lambda b, i: (b, i, 0)),
                  pl.BlockSpec((1, HALO, d), lambda b, i: (b, jnp.maximum(i * hb - 1, 0), 0)),
                  pl.BlockSpec((1, HALO, d), lambda b, i: (b, jnp.minimum((i + 1) * hb, n_hb - 1), 0)),
                  pl.BlockSpec((1, 6, d), lambda b, i: (b, 0, 0)),
                  _const_spec(ng.shape),
                  _const_spec(w_up.shape),
                  _const_spec(conv_w.shape),
                  _const_spec((1, ffn)),
                  _const_spec(w_down.shape)],
        out_specs=pl.BlockSpec((1, tile, d), lambda b, i: (b, i, 0)),
        out_shape=jax.ShapeDtypeStruct(x.shape, F32),
        scratch_shapes=[pltpu.VMEM((tile + 2 * HALO, d), BF16),
                        pltpu.VMEM((tile + 2 * HALO, ffn), F32)],
        compiler_params=_params(2),
        name="conv_glu",
    )(x, x, x, mod, ng, w_up, conv_w, conv_b.reshape(1, ffn), w_down)


def _qkv_kernel(x_ref, mod_ref, ng_ref, w_ref, b_ref, cos_ref, sin_ref, q_ref, k_ref, v_ref,
                *, q_dim, kv_dim):
    x = x_ref[0]
    mod = mod_ref[0]
    sh, sc = mod[0:1], mod[1:2]
    h = _rms(x, ng_ref[0:1]) * (1.0 + sc) + sh
    qkv = _dot(h.astype(BF16), w_ref[...]) + b_ref[...]
    cosf = cos_ref[...]
    sinf = sin_ref[...]
    lane = lax.broadcasted_iota(jnp.int32, cosf.shape, 1)
    low_half = (lane % HEAD_DIM) < (HEAD_DIM // 2)
    scale = HEAD_DIM ** -0.5
    for j in range((q_dim + kv_dim) // 128):
        blk = qkv[:, j * 128:(j + 1) * 128]
        partner = jnp.where(low_half, pltpu.roll(blk, 96, 1), pltpu.roll(blk, 32, 1))
        r = blk * cosf + partner * sinf
        if j * 128 < q_dim:
            q_ref[0, :, j * 128:(j + 1) * 128] = (r * scale).astype(BF16)
        else:
            k_ref[0, :, j * 128 - q_dim:(j + 1) * 128 - q_dim] = r.astype(BF16)
    v_ref[0] = qkv[:, q_dim + kv_dim:].astype(BF16)


def _qkv(x, mod, ng, w_qkv, b_qkv, cosf, sinf, tile=256):
    bsz, n_tok, d = x.shape
    q_dim = N_HEADS * HEAD_DIM
    kv_dim = N_KV_HEADS * HEAD_DIM
    n = q_dim + 2 * kv_dim
    kern = functools.partial(_qkv_kernel, q_dim=q_dim, kv_dim=kv_dim)
    return pl.pallas_call(
        kern,
        grid=(bsz, n_tok // tile),
        in_specs=[pl.BlockSpec((1, tile, d), lambda b, i: (b, i, 0)),
                  pl.BlockSpec((1, 6, d), lambda b, i: (b, 0, 0)),
                  _const_spec(ng.shape),
                  _const_spec(w_qkv.shape),
                  _const_spec((1, n)),
                  pl.BlockSpec((tile, 128), lambda b, i: (i, 0)),
                  pl.BlockSpec((tile, 128), lambda b, i: (i, 0))],
        out_specs=[pl.BlockSpec((1, tile, q_dim), lambda b, i: (b, i, 0)),
                   pl.BlockSpec((1, tile, kv_dim), lambda b, i: (b, i, 0)),
                   pl.BlockSpec((1, tile, kv_dim), lambda b, i: (b, i, 0))],
        out_shape=[jax.ShapeDtypeStruct((bsz, n_tok, q_dim), BF16),
                   jax.ShapeDtypeStruct((bsz, n_tok, kv_dim), BF16),
                   jax.ShapeDtypeStruct((bsz, n_tok, kv_dim), BF16)],
        compiler_params=_params(2),
        name="qkv_rope",
    )(x, mod, ng, w_qkv, b_qkv.reshape(1, n), cosf, sinf)


def _rope_tables(n_tok):
    rows = n_tok // GRID_W
    n_freq = HEAD_DIM // 4
    row = jnp.repeat(jnp.arange(rows), GRID_W).astype(F32)
    col = jnp.tile(jnp.arange(GRID_W), rows).astype(F32)
    inv_freq = ROPE_THETA ** (-(jnp.arange(n_freq, dtype=F32) / n_freq))
    ang = jnp.concatenate([row[:, None] * inv_freq, col[:, None] * inv_freq], axis=-1)
    cos, sin = jnp.cos(ang), jnp.sin(ang)
    cosf = jnp.tile(cos, (1, 4))
    sinf = jnp.tile(jnp.concatenate([-sin, sin], axis=-1), (1, 2))
    return cosf, sinf


def _attn_kernel(sink_ref, q_ref, kp_ref, kc_ref, kn_ref, vp_ref, vc_ref, vn_ref, kctx_ref, vctx_ref,
                 x_ref, mod_ref, ng_ref, wo_ref, o_ref, *, n_blk, n_ctx):
    j = pl.program_id(1)
    q = q_ref[0]
    kcat = jnp.concatenate([kp_ref[0], kc_ref[0], kn_ref[0], kctx_ref[0]], axis=0)
    vcat = jnp.concatenate([vp_ref[0], vc_ref[0], vn_ref[0], vctx_ref[0]], axis=0)
    n_keys = 3 * BLOCK + n_ctx
    r = lax.broadcasted_iota(jnp.int32, (BLOCK, n_keys), 0)
    c = lax.broadcasted_iota(jnp.int32, (BLOCK, n_keys), 1)
    ok_prev = (c < BLOCK) & (c >= r) & (j > 0)
    ok_next = (c >= 2 * BLOCK) & (c < 3 * BLOCK) & (c - 2 * BLOCK <= r) & (j < n_blk - 1)
    ok_rest = ((c >= BLOCK) & (c < 2 * BLOCK)) | (c >= 3 * BLOCK)
    valid = ok_prev | ok_next | ok_rest
    valid = jnp.concatenate([valid] * Q_PER_KV, axis=0)
    outs = [None] * N_HEADS
    for h in range(N_KV_HEADS):
        heads = [h * Q_PER_KV + g for g in range(Q_PER_KV)]
        qh = jnp.concatenate([q[:, n * HEAD_DIM:(n + 1) * HEAD_DIM] for n in heads], axis=0)
        kh = kcat[:, h * HEAD_DIM:(h + 1) * HEAD_DIM]
        vh = vcat[:, h * HEAD_DIM:(h + 1) * HEAD_DIM]
        s = lax.dot_general(qh, kh, (((1,), (1,)), ((), ())), preferred_element_type=F32)
        s = jnp.where(valid, s, -jnp.inf)
        sink = jnp.concatenate([jnp.full((BLOCK, 1), sink_ref[n], F32) for n in heads], axis=0)
        m = jnp.maximum(jnp.max(s, axis=-1, keepdims=True), sink)
        p = jnp.exp(s - m)
        denom = jnp.sum(p, axis=-1, keepdims=True) + jnp.exp(sink - m)
        o = _dot(p.astype(BF16), vh) / denom
        for g, n in enumerate(heads):
            outs[n] = o[g * BLOCK:(g + 1) * BLOCK]
    attn = jnp.concatenate(outs, axis=1).astype(BF16)
    y = _dot(attn, wo_ref[...])
    gate = mod_ref[0][2:3]
    o_ref[0] = x_ref[0] + gate * _rms(y, ng_ref[1:2])


def _attn(x, q, k, v, kctx, vctx, sink, mod, ng, w_o):
    bsz, n_tok, d = x.shape
    n_blk = n_tok // BLOCK
    n_ctx = kctx.shape[1]
    q_dim = q.shape[2]
    kv_dim = k.shape[2]
    kern = functools.partial(_attn_kernel, n_blk=n_blk, n_ctx=n_ctx)
    prev_map = lambda b, j: (b, jnp.maximum(j - 1, 0), 0)
    cur_map = lambda b, j: (b, j, 0)
    next_map = lambda b, j: (b, jnp.minimum(j + 1, n_blk - 1), 0)
    batch_map = lambda b, j: (b, 0, 0)
    kv_spec = lambda m: pl.BlockSpec((1, BLOCK, kv_dim), m)
    return pl.pallas_call(
        kern,
        grid=(bsz, n_blk),
        in_specs=[pl.BlockSpec(memory_space=pltpu.SMEM),
                  pl.BlockSpec((1, BLOCK, q_dim), cur_map),
                  kv_spec(prev_map), kv_spec(cur_map), kv_spec(next_map),
                  kv_spec(prev_map), kv_spec(cur_map), kv_spec(next_map),
                  pl.BlockSpec((1, n_ctx, kv_dim), batch_map),
                  pl.BlockSpec((1, n_ctx, kv_dim), batch_map),
                  pl.BlockSpec((1, BLOCK, d), cur_map),
                  pl.BlockSpec((1, 6, d), batch_map),
                  _const_spec(ng.shape),
                  _const_spec(w_o.shape)],
        out_specs=pl.BlockSpec((1, BLOCK, d), cur_map),
        out_shape=jax.ShapeDtypeStruct(x.shape, F32),
        compiler_params=_params(2),
        name="window_attn",
    )(sink, q, k, k, k, v, v, v, kctx, vctx, x, mod, ng, w_o)


def kernel(x, c, ctx, c_ctx, ada_w, ada_b, norm_g, a_w_in, a_ln_g, a_ln_b, a_w_s, a_b_s, a_w_out,
           b_w_qkv, b_b_qkv, b_sink, b_w_o, f_w_up, f_conv_w, f_conv_b, f_w_down):
    bsz, n_tok, d = x.shape
    n_ctx = ctx.shape[1]
    depth = ada_w.shape[0]
    assert depth == 2, "layer 0 = gMLP mixer, layer 1 = windowed attention (last layer: no context update)"

    rows = 8 * ((bsz + 1 + 7) // 8)
    cond = jnp.zeros((rows, d), F32).at[:bsz].set(c).at[bsz].set(c_ctx)
    mods = _ada(cond, ada_w, ada_b)

    def lat_mod(i):
        return mods[i, :bsz].reshape(bsz, 6, d)

    def ctx_mod(i):
        return jnp.broadcast_to(mods[i, bsz].reshape(1, 6, d), (bsz, 6, d))

    bf = lambda w: w.astype(BF16)

    gm = (norm_g[0], bf(a_w_in[0]), a_ln_g[0], a_ln_b[0], bf(a_w_s[0]), a_b_s[0], bf(a_w_out[0]))
    ff0 = (norm_g[0], bf(f_w_up[0]), f_conv_w[0], f_conv_b[0], bf(f_w_down[0]))
    x = _gmlp(x, lat_mod(0), *gm)
    ctx = _gmlp(ctx, ctx_mod(0), *gm)
    x = _ffn(x, lat_mod(0), *ff0)
    ctx = _ffn(ctx, ctx_mod(0), *ff0)

    cosf, sinf = _rope_tables(n_tok)
    w_qkv = bf(b_w_qkv[0])
    q, k, v = _qkv(x, lat_mod(1), norm_g[1], w_qkv, b_b_qkv[0], cosf, sinf)
    _, kctx, vctx = _qkv(ctx, ctx_mod(1), norm_g[1], w_qkv, b_b_qkv[0],
                         jnp.ones((n_ctx, 128), F32), jnp.zeros((n_ctx, 128), F32))
    x = _attn(x, q, k, v, kctx, vctx, b_sink[0], lat_mod(1), norm_g[1], bf(b_w_o[0]))
    x = _ffn(x, lat_mod(1), norm_g[1], bf(f_w_up[1]), f_conv_w[1], f_conv_b[1], bf(f_w_down[1]))
    return x
```

```python
import functools
import math

import jax
import jax.numpy as jnp
from jax import lax
from jax.experimental import pallas as pl
from jax.experimental.pallas import tpu as pltpu

EPS = 1e-6
GRID_W = 64
CHUNK = 128
A_GROUPS = 8
HEAD_DIM = 64
N_HEADS = 16
N_KV_HEADS = 4
Q_PER_KV = N_HEADS // N_KV_HEADS
BLOCK = 128
ROPE_THETA = 10000.0
LOG2E = math.log2(math.e)
SM_ROWS = 16
HALO = 16
VMEM_LIMIT = 56 * 1024 * 1024

F32 = jnp.float32
BF16 = jnp.bfloat16


def _rms(x, g):
    return x * lax.rsqrt(jnp.mean(x * x, axis=-1, keepdims=True) + EPS) * g


def _gelu(x):
    c = math.sqrt(2.0 / math.pi)
    return x * (0.5 * (1.0 + jnp.tanh(c * (x + 0.044715 * (x * x * x)))))


def _dot(a, b):
    return jnp.dot(a, b, preferred_element_type=F32)


def _params(n_grid):
    return pltpu.CompilerParams(dimension_semantics=("parallel",) * n_grid,
                                vmem_limit_bytes=VMEM_LIMIT)


def _const_spec(shape):
    nd = len(shape)
    return pl.BlockSpec(shape, lambda *_: (0,) * nd, pipeline_mode=pl.Buffered(1))


def _ada_kernel(s_ref, w_ref, b_ref, o_ref):
    s = s_ref[...]
    s = s * jax.nn.sigmoid(s)
    o_ref[0] = _dot(s.astype(BF16), w_ref[0].astype(BF16)) + b_ref[0]


def _ada(cond, ada_w, ada_b, tn=1536):
    depth, d, n = ada_w.shape
    r = cond.shape[0]
    return pl.pallas_call(
        _ada_kernel,
        grid=(depth, n // tn),
        in_specs=[pl.BlockSpec((r, d), lambda i, j: (0, 0)),
                  pl.BlockSpec((1, d, tn), lambda i, j: (i, 0, j)),
                  pl.BlockSpec((1, 1, tn), lambda i, j: (i, 0, j))],
        out_specs=pl.BlockSpec((1, r, tn), lambda i, j: (i, 0, j)),
        out_shape=jax.ShapeDtypeStruct((depth, r, n), F32),
        compiler_params=_params(2),
        name="ada_mod",
    )(cond, ada_w, ada_b.reshape(depth, 1, n))


def _gmlp_kernel(x_ref, mod_ref, ng_ref, w_in_ref, lng_ref, lnb_ref, ws_ref, bs_ref, w_out_ref,
                 o_ref, t_ref, *, tile, width):
    x = x_ref[0]
    mod = mod_ref[0]
    sh, sc, gate = mod[0:1], mod[1:2], mod[2:3]
    h = _rms(x, ng_ref[0:1]) * (1.0 + sc) + sh
    z = _gelu(_dot(h.astype(BF16), w_in_ref[...]))
    u = z[:, :width]
    v = z[:, width:]
    mu = jnp.mean(v, axis=-1, keepdims=True)
    vc = v - mu
    var = jnp.mean(vc * vc, axis=-1, keepdims=True)
    vn = (vc * lax.rsqrt(var + EPS) * lng_ref[...] + lnb_ref[...]).astype(BF16)
    gw = width // A_GROUPS
    for c in range(tile // CHUNK):
        rows = slice(c * CHUNK, (c + 1) * CHUNK)
        for g in range(A_GROUPS):
            cols = slice(g * gw, (g + 1) * gw)
            s = _dot(ws_ref[g], vn[rows, cols]) + bs_ref[g]
            t_ref[rows, cols] = (u[rows, cols] * s).astype(BF16)
    y = _dot(t_ref[...], w_out_ref[...])
    o_ref[0] = x + gate * _rms(y, ng_ref[1:2])


def _gmlp(x, mod, ng, w_in, ln_g, ln_b, w_s, b_s, w_out, tile=256):
    bsz, n_tok, d = x.shape
    width = w_out.shape[0]
    gw = width // A_GROUPS
    bs_full = jnp.broadcast_to(b_s[:, :, None], (A_GROUPS, CHUNK, gw))
    kern = functools.partial(_gmlp_kernel, tile=tile, width=width)
    return pl.pallas_call(
        kern,
        grid=(bsz, n_tok // tile),
        in_specs=[pl.BlockSpec((1, tile, d), lambda b, i: (b, i, 0)),
                  pl.BlockSpec((1, 6, d), lambda b, i: (b, 0, 0)),
                  _const_spec(ng.shape),
                  _const_spec(w_in.shape),
                  _const_spec((1, width)),
                  _const_spec((1, width)),
                  _const_spec(w_s.shape),
                  _const_spec(bs_full.shape),
                  _const_spec(w_out.shape)],
        out_specs=pl.BlockSpec((1, tile, d), lambda b, i: (b, i, 0)),
        out_shape=jax.ShapeDtypeStruct(x.shape, F32),
        scratch_shapes=[pltpu.VMEM((tile, width), BF16)],
        compiler_params=_params(2),
        name="gmlp_mixer",
    )(x, mod, ng, w_in, ln_g.reshape(1, width), ln_b.reshape(1, width), w_s, bs_full, w_out)


def _ffn_kernel(x_ref, xp_ref, xn_ref, mod_ref, ng_ref, w_up_ref, cw_ref, cb_ref, w_dn_ref,
                o_ref, h_ref, a_ref, *, tile, n_tiles, ffn):
    i = pl.program_id(1)
    mod = mod_ref[0]
    sh, sc, gate = mod[3:4], mod[4:5], mod[5:6]
    g_pre = ng_ref[2:3]

    def pre(xx):
        return _rms(xx, g_pre) * (1.0 + sc) + sh

    x = x_ref[0]
    has_prev = (i > 0).astype(F32)
    has_next = (i < n_tiles - 1).astype(F32)
    h_ref[0:HALO] = (pre(xp_ref[0]) * has_prev).astype(BF16)
    h_ref[HALO:HALO + tile] = pre(x).astype(BF16)
    h_ref[HALO + tile:] = (pre(xn_ref[0]) * has_next).astype(BF16)
    a_ref[...] = _dot(h_ref[...], w_up_ref[:, :ffn])
    b = _dot(h_ref[HALO:HALO + tile], w_up_ref[:, ffn:])
    a = (a_ref[HALO - 1:HALO - 1 + tile] * cw_ref[0:1]
         + a_ref[HALO:HALO + tile] * cw_ref[1:2]
         + a_ref[HALO + 1:HALO + 1 + tile] * cw_ref[2:3]
         + cb_ref[...])
    y = _dot((_gelu(a) * b).astype(BF16), w_dn_ref[...])
    o_ref[0] = x + gate * _rms(y, ng_ref[3:4])


def _ffn(x, mod, ng, w_up, conv_w, conv_b, w_down, tile=256):
    bsz, n_tok, d = x.shape
    ffn = w_down.shape[0]
    n_tiles = n_tok // tile
    hb = tile // HALO
    n_hb = n_tok // HALO
    kern = functools.partial(_ffn_kernel, tile=tile, n_tiles=n_tiles, ffn=ffn)
    return pl.pallas_call(
        kern,
        grid=(bsz, n_tiles),
        in_specs=[pl.BlockSpec((1, tile, d), lambda b, i: (b, i, 0)),
                  pl.BlockSpec((1, HALO, d), lambda b, i: (b, jnp.maximum(i * hb - 1, 0), 0)),
                  pl.BlockSpec((1, HALO, d), lambda b, i: (b, jnp.minimum((i + 1) * hb, n_hb - 1), 0)),
                  pl.BlockSpec((1, 6, d), lambda b, i: (b, 0, 0)),
                  _const_spec(ng.shape),
                  _const_spec(w_up.shape),
                  _const_spec(conv_w.shape),
                  _const_spec((1, ffn)),
                  _const_spec(w_down.shape)],
        out_specs=pl.BlockSpec((1, tile, d), lambda b, i: (b, i, 0)),
        out_shape=jax.ShapeDtypeStruct(x.shape, F32),
        scratch_shapes=[pltpu.VMEM((tile + 2 * HALO, d), BF16),
                        pltpu.VMEM((tile + 2 * HALO, ffn), F32)],
        compiler_params=_params(2),
        name="conv_glu",
    )(x, x, x, mod, ng, w_up, conv_w, conv_b.reshape(1, ffn), w_down)


def _qkv_kernel(x_ref, mod_ref, ng_ref, w_ref, b_ref, cos_ref, sin_ref, q_ref, kt_ref, v_ref,
                *, q_dim, kv_dim):
    x = x_ref[0]
    mod = mod_ref[0]
    sh, sc = mod[0:1], mod[1:2]
    h = _rms(x, ng_ref[0:1]) * (1.0 + sc) + sh
    qkv = _dot(h.astype(BF16), w_ref[...]) + b_ref[...]
    cosf = cos_ref[...]
    sinf = sin_ref[...]
    lane = lax.broadcasted_iota(jnp.int32, cosf.shape, 1)
    low_half = (lane % HEAD_DIM) < (HEAD_DIM // 2)
    scale = HEAD_DIM ** -0.5 * LOG2E
    for j in range((q_dim + kv_dim) // 128):
        blk = qkv[:, j * 128:(j + 1) * 128]
        partner = jnp.where(low_half, pltpu.roll(blk, 96, 1), pltpu.roll(blk, 32, 1))
        r = blk * cosf + partner * sinf
        if j * 128 < q_dim:
            q_ref[0, :, j * 128:(j + 1) * 128] = (r * scale).astype(BF16)
        else:
            kt_ref[0, j * 128 - q_dim:(j + 1) * 128 - q_dim, :] = r.T.astype(BF16)
    v_ref[0] = qkv[:, q_dim + kv_dim:].astype(BF16)


def _qkv(x, mod, ng, w_qkv, b_qkv, cosf, sinf, tile=256):
    bsz, n_tok, d = x.shape
    q_dim = N_HEADS * HEAD_DIM
    kv_dim = N_KV_HEADS * HEAD_DIM
    n = q_dim + 2 * kv_dim
    kern = functools.partial(_qkv_kernel, q_dim=q_dim, kv_dim=kv_dim)
    return pl.pallas_call(
        kern,
        grid=(bsz, n_tok // tile),
        in_specs=[pl.BlockSpec((1, tile, d), lambda b, i: (b, i, 0)),
                  pl.BlockSpec((1, 6, d), lambda b, i: (b, 0, 0)),
                  _const_spec(ng.shape),
                  _const_spec(w_qkv.shape),
                  _const_spec((1, n)),
                  pl.BlockSpec((tile, 128), lambda b, i: (i, 0)),
                  pl.BlockSpec((tile, 128), lambda b, i: (i, 0))],
        out_specs=[pl.BlockSpec((1, tile, q_dim), lambda b, i: (b, i, 0)),
                   pl.BlockSpec((1, kv_dim, tile), lambda b, i: (b, 0, i)),
                   pl.BlockSpec((1, tile, kv_dim), lambda b, i: (b, i, 0))],
        out_shape=[jax.ShapeDtypeStruct((bsz, n_tok, q_dim), BF16),
                   jax.ShapeDtypeStruct((bsz, kv_dim, n_tok), BF16),
                   jax.ShapeDtypeStruct((bsz, n_tok, kv_dim), BF16)],
        compiler_params=_params(2),
        name="qkv_rope",
    )(x, mod, ng, w_qkv, b_qkv.reshape(1, n), cosf, sinf)


def _rope_tables(n_tok):
    rows = n_tok // GRID_W
    n_freq = HEAD_DIM // 4
    row = jnp.repeat(jnp.arange(rows), GRID_W).astype(F32)
    col = jnp.tile(jnp.arange(GRID_W), rows).astype(F32)
    inv_freq = ROPE_THETA ** (-(jnp.arange(n_freq, dtype=F32) / n_freq))
    ang = jnp.concatenate([row[:, None] * inv_freq, col[:, None] * inv_freq], axis=-1)
    cos, sin = jnp.cos(ang), jnp.sin(ang)
    cosf = jnp.tile(cos, (1, 4))
    sinf = jnp.tile(jnp.concatenate([-sin, sin], axis=-1), (1, 2))
    return cosf, sinf


def _attn_kernel(sink_ref, q_ref, ktp_ref, ktc_ref, ktn_ref, vp_ref, vc_ref, vn_ref, kctx_ref, vctx_ref,
                 x_ref, mod_ref, ng_ref, wo_ref, o_ref, kwin_ref, vwin_ref, attn_ref, bias_ref, s_ref, p_ref,
                 m_ref, es_ref, ve_ref, vo_ref, vce_ref, vco_ref,
                 *, tq, n_tiles):
    i = pl.program_id(1)
    nb = tq // BLOCK
    kwin_ref[:, 0:BLOCK] = ktp_ref[0]
    kwin_ref[:, BLOCK:BLOCK + tq] = ktc_ref[0]
    kwin_ref[:, BLOCK + tq:] = ktn_ref[0]
    vwin_ref[0:BLOCK] = vp_ref[0]
    vwin_ref[BLOCK:BLOCK + tq] = vc_ref[0]
    vwin_ref[BLOCK + tq:] = vn_ref[0]
    band_w = 3 * BLOCK
    r = lax.broadcasted_iota(jnp.int32, (BLOCK, band_w), 0)
    c = lax.broadcasted_iota(jnp.int32, (BLOCK, band_w), 1)
    band = (c >= r) & (c <= r + 2 * BLOCK)
    first = band & ((c >= BLOCK) | (i > 0))
    last = band & ((c < 2 * BLOCK) | (i < n_tiles - 1))
    neg = jnp.float32(-jnp.inf)
    bias_ref[0, 0] = jnp.where(first, 0.0, neg)[:, 0:BLOCK]
    bias_ref[0, 1] = jnp.where(band, 0.0, neg)[:, 0:BLOCK]
    bias_ref[1, 0] = jnp.where(last, 0.0, neg)[:, 2 * BLOCK:]
    bias_ref[1, 1] = jnp.where(band, 0.0, neg)[:, 2 * BLOCK:]

    lane = lax.broadcasted_iota(jnp.int32, (1, 2 * HEAD_DIM), 1)
    low = lane < HEAD_DIM
    one = jnp.ones((), BF16)

    def pad_values(src_ref, even_ref, odd_ref, n_rows):
        for t in range(N_KV_HEADS // 2):
            src = src_ref[0:n_rows, t * 128:(t + 1) * 128]
            swapped = pltpu.roll(src, HEAD_DIM, 1)
            even_ref[0:n_rows, (2 * t) * 128:(2 * t + 1) * 128] = jnp.where(low, src, one)
            odd_ref[0:n_rows, (2 * t) * 128:(2 * t + 1) * 128] = jnp.where(low, one, swapped)
            even_ref[0:n_rows, (2 * t + 1) * 128:(2 * t + 2) * 128] = jnp.where(low, swapped, one)
            odd_ref[0:n_rows, (2 * t + 1) * 128:(2 * t + 2) * 128] = jnp.where(low, one, src)

    pad_values(vwin_ref, ve_ref, vo_ref, tq + 2 * BLOCK)
    pad_values(vctx_ref.at[0], vce_ref, vco_ref, vctx_ref.shape[1])

    n_rows = Q_PER_KV * BLOCK
    half = n_rows // 2
    row_order = (0, 2, 1, 3)
    unit = 0
    for b in range(nb):
        bp = 0 if b == 0 else 1
        bn = 0 if b == nb - 1 else 1
        q = q_ref[0, b * BLOCK:(b + 1) * BLOCK, :]
        for h in range(N_KV_HEADS):
            par = unit % 2
            unit += 1
            heads = [h * Q_PER_KV + g for g in row_order]
            feat = slice(h * HEAD_DIM, (h + 1) * HEAD_DIM)
            tile = slice(h * 128, (h + 1) * 128)
            keys = slice(b * BLOCK, b * BLOCK + band_w)
            qh = jnp.concatenate([q[:, n * HEAD_DIM:(n + 1) * HEAD_DIM] for n in heads], axis=0)
            s_ref[par, :, 0:band_w] = _dot(qh, kwin_ref[feat, keys])
            s_ref[par, :, band_w:] = _dot(qh, kctx_ref[0, feat, :])
            for ci in range(n_rows // SM_ROWS):
                rows = slice(ci * SM_ROWS, (ci + 1) * SM_ROWS)
                r0 = (ci * SM_ROWS) % BLOCK
                sink = sink_ref[heads[ci * SM_ROWS // BLOCK]] * LOG2E
                mt = jnp.maximum(s_ref[par, rows, 0:BLOCK] + bias_ref[0, bp, r0:r0 + SM_ROWS, :],
                                 s_ref[par, rows, 2 * BLOCK:band_w] + bias_ref[1, bn, r0:r0 + SM_ROWS, :])
                for t in (1, 3, 4):
                    mt = jnp.maximum(mt, s_ref[par, rows, t * BLOCK:(t + 1) * BLOCK])
                m = jnp.maximum(jnp.max(mt, axis=-1, keepdims=True), sink)
                m_ref[par, rows, :] = jnp.broadcast_to(m, (SM_ROWS, BLOCK))
                es_ref[par, rows, :] = jnp.broadcast_to(jnp.exp2(sink - m), (SM_ROWS, BLOCK))
            for ci in range(n_rows // SM_ROWS):
                rows = slice(ci * SM_ROWS, (ci + 1) * SM_ROWS)
                r0 = (ci * SM_ROWS) % BLOCK
                m = m_ref[par, rows, :]
                for t in range(5):
                    st = s_ref[par, rows, t * BLOCK:(t + 1) * BLOCK]
                    if t == 0:
                        st = st + bias_ref[0, bp, r0:r0 + SM_ROWS, :]
                    if t == 2:
                        st = st + bias_ref[1, bn, r0:r0 + SM_ROWS, :]
                    p_ref[par, rows, t * BLOCK:(t + 1) * BLOCK] = jnp.exp2(st - m).astype(BF16)
            for part, (v_ref, vc_ref) in enumerate(((ve_ref, vce_ref), (vo_ref, vco_ref))):
                rows = slice(part * half, (part + 1) * half)
                acc = (_dot(p_ref[par, rows, 0:band_w], v_ref[keys, tile])
                       + _dot(p_ref[par, rows, band_w:], vc_ref[:, tile]))
                denom = pltpu.roll(acc, HEAD_DIM, 1) + es_ref[par, rows, :]
                o_ref_half = acc / denom
                if part == 0:
                    o_even = o_ref_half
                else:
                    o_odd = o_ref_half
            for gp in range(Q_PER_KV // 2):
                pair = jnp.where(low, o_even[gp * BLOCK:(gp + 1) * BLOCK], o_odd[gp * BLOCK:(gp + 1) * BLOCK])
                col = (h * Q_PER_KV + 2 * gp) * HEAD_DIM
                attn_ref[b * BLOCK:(b + 1) * BLOCK, col:col + 2 * HEAD_DIM] = pair.astype(BF16)
    y = _dot(attn_ref[...], wo_ref[...])
    gate = mod_ref[0][2:3]
    o_ref[0] = x_ref[0] + gate * _rms(y, ng_ref[1:2])


def _attn(x, q, kt, v, kctx_t, vctx, sink, mod, ng, w_o, tq=512):
    bsz, n_tok, d = x.shape
    n_tiles = n_tok // tq
    nb = tq // BLOCK
    n_blk = n_tok // BLOCK
    n_ctx = vctx.shape[1]
    q_dim = q.shape[2]
    kv_dim = v.shape[2]
    kern = functools.partial(_attn_kernel, tq=tq, n_tiles=n_tiles)
    cur_map = lambda b, i: (b, i, 0)
    batch_map = lambda b, i: (b, 0, 0)
    return pl.pallas_call(
        kern,
        grid=(bsz, n_tiles),
        in_specs=[pl.BlockSpec(memory_space=pltpu.SMEM),
                  pl.BlockSpec((1, tq, q_dim), cur_map),
                  pl.BlockSpec((1, kv_dim, BLOCK), lambda b, i: (b, 0, jnp.maximum(i * nb - 1, 0))),
                  pl.BlockSpec((1, kv_dim, tq), lambda b, i: (b, 0, i)),
                  pl.BlockSpec((1, kv_dim, BLOCK), lambda b, i: (b, 0, jnp.minimum((i + 1) * nb, n_blk - 1))),
                  pl.BlockSpec((1, BLOCK, kv_dim), lambda b, i: (b, jnp.maximum(i * nb - 1, 0), 0)),
                  pl.BlockSpec((1, tq, kv_dim), cur_map),
                  pl.BlockSpec((1, BLOCK, kv_dim), lambda b, i: (b, jnp.minimum((i + 1) * nb, n_blk - 1), 0)),
                  pl.BlockSpec((1, kv_dim, n_ctx), batch_map),
                  pl.BlockSpec((1, n_ctx, kv_dim), batch_map),
                  pl.BlockSpec((1, tq, d), cur_map),
                  pl.BlockSpec((1, 6, d), batch_map),
                  _const_spec(ng.shape),
                  _const_spec(w_o.shape)],
        out_specs=pl.BlockSpec((1, tq, d), cur_map),
        out_shape=jax.ShapeDtypeStruct(x.shape, F32),
        scratch_shapes=[pltpu.VMEM((kv_dim, tq + 2 * BLOCK), BF16),
                        pltpu.VMEM((tq + 2 * BLOCK, kv_dim), BF16),
                        pltpu.VMEM((tq, q_dim), BF16),
                        pltpu.VMEM((2, 2, BLOCK, BLOCK), F32),
                        pltpu.VMEM((2, Q_PER_KV * BLOCK, 3 * BLOCK + n_ctx), F32),
                        pltpu.VMEM((2, Q_PER_KV * BLOCK, 3 * BLOCK + n_ctx), BF16),
                        pltpu.VMEM((2, Q_PER_KV * BLOCK, BLOCK), F32),
                        pltpu.VMEM((2, Q_PER_KV * BLOCK, BLOCK), F32),
                        pltpu.VMEM((tq + 2 * BLOCK, N_KV_HEADS * 128), BF16),
                        pltpu.VMEM((tq + 2 * BLOCK, N_KV_HEADS * 128), BF16),
                        pltpu.VMEM((n_ctx, N_KV_HEADS * 128), BF16),
                        pltpu.VMEM((n_ctx, N_KV_HEADS * 128), BF16)],
        compiler_params=_params(2),
        name="window_attn",
    )(sink, q, kt, kt, kt, v, v, v, kctx_t, vctx, x, mod, ng, w_o)


def kernel(x, c, ctx, c_ctx, ada_w, ada_b, norm_g, a_w_in, a_ln_g, a_ln_b, a_w_s, a_b_s, a_w_out,
           b_w_qkv, b_b_qkv, b_sink, b_w_o, f_w_up, f_conv_w, f_conv_b, f_w_down):
    bsz, n_tok, d = x.shape
    n_ctx = ctx.shape[1]
    depth = ada_w.shape[0]
    assert depth == 2, "layer 0 = gMLP mixer, layer 1 = windowed attention (last layer: no context update)"

    rows = 8 * ((bsz + 1 + 7) // 8)
    cond = jnp.zeros((rows, d), F32).at[:bsz].set(c).at[bsz].set(c_ctx)
    mods = _ada(cond, ada_w, ada_b)

    def lat_mod(i):
        return mods[i, :bsz].reshape(bsz, 6, d)

    def ctx_mod(i):
        return jnp.broadcast_to(mods[i, bsz].reshape(1, 6, d), (bsz, 6, d))

    bf = lambda w: w.astype(BF16)

    gm = (norm_g[0], bf(a_w_in[0]), a_ln_g[0], a_ln_b[0], bf(a_w_s[0]), a_b_s[0], bf(a_w_out[0]))
    ff0 = (norm_g[0], bf(f_w_up[0]), f_conv_w[0], f_conv_b[0], bf(f_w_down[0]))
    x = _gmlp(x, lat_mod(0), *gm)
    ctx = _gmlp(ctx, ctx_mod(0), *gm)
    x = _ffn(x, lat_mod(0), *ff0)
    ctx = _ffn(ctx, ctx_mod(0), *ff0)

    cosf, sinf = _rope_tables(n_tok)
    w_qkv = bf(b_w_qkv[0])
    q, k, v = _qkv(x, lat_mod(1), norm_g[1], w_qkv, b_b_qkv[0], cosf, sinf)
    _, kctx, vctx = _qkv(ctx, ctx_mod(1), norm_g[1], w_qkv, b_b_qkv[0],
                         jnp.ones((n_ctx, 128), F32), jnp.zeros((n_ctx, 128), F32))
    x = _attn(x, q, k, v, kctx, vctx, b_sink[0], lat_mod(1), norm_g[1], bf(b_w_o[0]))
    x = _ffn(x, lat_mod(1), norm_g[1], bf(f_w_up[1]), f_conv_w[1], f_conv_b[1], bf(f_w_down[1]))
    return x
```

```python
import functools
import math

import jax
import jax.numpy as jnp
from jax import lax
from jax.experimental import pallas as pl
from jax.experimental.pallas import tpu as pltpu

EPS = 1e-6
GRID_W = 64
CHUNK = 128
A_GROUPS = 8
HEAD_DIM = 64
N_HEADS = 16
N_KV_HEADS = 4
Q_PER_KV = N_HEADS // N_KV_HEADS
BLOCK = 128
ROPE_THETA = 10000.0
LOG2E = math.log2(math.e)
SM_ROWS = 16
GMLP_CHUNK = 512
GMLP_OUT_GROUPS = 4
FFN_CHUNK = 1024
HALO = 16
VMEM_LIMIT = 56 * 1024 * 1024

F32 = jnp.float32
BF16 = jnp.bfloat16


def _rms(x, g):
    return x * lax.rsqrt(jnp.mean(x * x, axis=-1, keepdims=True) + EPS) * g


def _gelu(x):
    c = math.sqrt(2.0 / math.pi)
    return x * (0.5 * (1.0 + jnp.tanh(c * (x + 0.044715 * (x * x * x)))))


def _dot(a, b):
    return jnp.dot(a, b, preferred_element_type=F32)


def _params(n_grid):
    return pltpu.CompilerParams(dimension_semantics=("parallel",) * n_grid,
                                vmem_limit_bytes=VMEM_LIMIT)


def _const_spec(shape):
    nd = len(shape)
    return pl.BlockSpec(shape, lambda *_: (0,) * nd, pipeline_mode=pl.Buffered(1))


def _ada_kernel(s_ref, w_ref, b_ref, o_ref):
    s = s_ref[...]
    s = s * jax.nn.sigmoid(s)
    o_ref[0] = _dot(s.astype(BF16), w_ref[0].astype(BF16)) + b_ref[0]


def _ada(cond, ada_w, ada_b, tn=1536):
    depth, d, n = ada_w.shape
    r = cond.shape[0]
    return pl.pallas_call(
        _ada_kernel,
        grid=(depth, n // tn),
        in_specs=[pl.BlockSpec((r, d), lambda i, j: (0, 0)),
                  pl.BlockSpec((1, d, tn), lambda i, j: (i, 0, j)),
                  pl.BlockSpec((1, 1, tn), lambda i, j: (i, 0, j))],
        out_specs=pl.BlockSpec((1, r, tn), lambda i, j: (i, 0, j)),
        out_shape=jax.ShapeDtypeStruct((depth, r, n), F32),
        compiler_params=_params(2),
        name="ada_mod",
    )(cond, ada_w, ada_b.reshape(depth, 1, n))


def _gmlp_kernel(x_ref, mod_ref, ng_ref, w_in_ref, lng_ref, lnb_ref, ws_ref, bs_ref, w_out_ref,
                 o_ref, h_ref, v_ref, t_ref, *, tile, width):
    x = x_ref[0]
    mod = mod_ref[0]
    sh, sc, gate = mod[0:1], mod[1:2], mod[2:3]
    h_ref[...] = (_rms(x, ng_ref[0:1]) * (1.0 + sc) + sh).astype(BF16)
    half = tile // 2 if tile >= 2 * CHUNK else tile

    def dot_rows(lhs_ref, w):
        return jnp.concatenate([_dot(lhs_ref[r0:r0 + half], w) for r0 in range(0, tile, half)], axis=0)

    vsum = jnp.zeros((tile, 1), F32)
    for c0 in range(0, width, GMLP_CHUNK):
        vj = _gelu(dot_rows(h_ref, w_in_ref[:, width + c0:width + c0 + GMLP_CHUNK]))
        v_ref[:, c0:c0 + GMLP_CHUNK] = vj
        vsum = vsum + jnp.sum(vj, axis=-1, keepdims=True)
    mu = vsum * (1.0 / width)
    vss = jnp.zeros((tile, 1), F32)
    for c0 in range(0, width, GMLP_CHUNK):
        dv = v_ref[:, c0:c0 + GMLP_CHUNK] - mu
        vss = vss + jnp.sum(dv * dv, axis=-1, keepdims=True)
    rstd = lax.rsqrt(vss * (1.0 / width) + EPS)
    gw = width // A_GROUPS
    y = None
    for g in range(A_GROUPS):
        cols = slice(g * gw, (g + 1) * gw)
        vn = ((v_ref[:, cols] - mu) * rstd * lng_ref[:, cols] + lnb_ref[:, cols]).astype(BF16)
        u = _gelu(dot_rows(h_ref, w_in_ref[:, cols]))
        s = jnp.concatenate([_dot(ws_ref[g], vn[r0:r0 + CHUNK]) + bs_ref[g]
                             for r0 in range(0, tile, CHUNK)], axis=0)
        t_ref[:, cols] = (u * s).astype(BF16)
        if (g + 1) % GMLP_OUT_GROUPS == 0:
            k0 = (g + 1 - GMLP_OUT_GROUPS) * gw
            part = dot_rows(t_ref.at[:, k0:(g + 1) * gw], w_out_ref[k0:(g + 1) * gw, :])
            y = part if y is None else y + part
    o_ref[0] = x + gate * _rms(y, ng_ref[1:2])


def _gmlp(x, mod, ng, w_in, ln_g, ln_b, w_s, b_s, w_out, tile=512):
    bsz, n_tok, d = x.shape
    width = w_out.shape[0]
    gw = width // A_GROUPS
    tile = min(tile, n_tok)
    bs_full = jnp.broadcast_to(b_s[:, :, None], (A_GROUPS, CHUNK, gw))
    kern = functools.partial(_gmlp_kernel, tile=tile, width=width)
    return pl.pallas_call(
        kern,
        grid=(bsz, n_tok // tile),
        in_specs=[pl.BlockSpec((1, tile, d), lambda b, i: (b, i, 0)),
                  pl.BlockSpec((1, 6, d), lambda b, i: (b, 0, 0)),
                  _const_spec(ng.shape),
                  _const_spec(w_in.shape),
                  _const_spec((1, width)),
                  _const_spec((1, width)),
                  _const_spec(w_s.shape),
                  _const_spec(bs_full.shape),
                  _const_spec(w_out.shape)],
        out_specs=pl.BlockSpec((1, tile, d), lambda b, i: (b, i, 0)),
        out_shape=jax.ShapeDtypeStruct(x.shape, F32),
        scratch_shapes=[pltpu.VMEM((tile, d), BF16),
                        pltpu.VMEM((tile, width), F32),
                        pltpu.VMEM((tile, width), BF16)],
        compiler_params=_params(2),
        name="gmlp_mixer",
    )(x, mod, ng, w_in, ln_g.reshape(1, width), ln_b.reshape(1, width), w_s, bs_full, w_out)


def _ffn_kernel(x_ref, xp_ref, xn_ref, mod_ref, ng_ref, w_up_ref, cw_ref, cb_ref, w_dn_ref,
                o_ref, h_ref, *, tile, n_tiles, ffn):
    i = pl.program_id(1)
    mod = mod_ref[0]
    sh, sc, gate = mod[3:4], mod[4:5], mod[5:6]
    g_pre = ng_ref[2:3]

    def pre(xx):
        return _rms(xx, g_pre) * (1.0 + sc) + sh

    x = x_ref[0]
    has_prev = (i > 0).astype(F32)
    has_next = (i < n_tiles - 1).astype(F32)
    h_ref[0:HALO] = (pre(xp_ref[0]) * has_prev).astype(BF16)
    h_ref[HALO:HALO + tile] = pre(x).astype(BF16)
    h_ref[HALO + tile:] = (pre(xn_ref[0]) * has_next).astype(BF16)
    ext = tile + 2 * HALO
    y = None
    c0 = 0
    while c0 < ffn:
        cw = min(FFN_CHUNK, ffn - c0)
        cols = slice(c0, c0 + cw)
        mid = ext // 2
        a_ext = jnp.concatenate([_dot(h_ref[0:mid], w_up_ref[:, cols]),
                                 _dot(h_ref[mid:ext], w_up_ref[:, cols])], axis=0)
        a = (pltpu.roll(a_ext, 1, 0)[HALO:HALO + tile] * cw_ref[0:1, cols]
             + a_ext[HALO:HALO + tile] * cw_ref[1:2, cols]
             + pltpu.roll(a_ext, ext - 1, 0)[HALO:HALO + tile] * cw_ref[2:3, cols]
             + cb_ref[:, cols])
        w_b = w_up_ref[:, ffn + c0:ffn + c0 + cw]
        b = jnp.concatenate([_dot(h_ref[HALO:mid], w_b), _dot(h_ref[mid:HALO + tile], w_b)], axis=0)
        g = (_gelu(a) * b).astype(BF16)
        part = jnp.concatenate([_dot(g[0:mid - HALO], w_dn_ref[cols, :]),
                                _dot(g[mid - HALO:tile], w_dn_ref[cols, :])], axis=0)
        y = part if y is None else y + part
        c0 += cw
    o_ref[0] = x + gate * _rms(y, ng_ref[3:4])


def _ffn(x, mod, ng, w_up, conv_w, conv_b, w_down, tile=512):
    bsz, n_tok, d = x.shape
    ffn = w_down.shape[0]
    tile = min(tile, n_tok)
    n_tiles = n_tok // tile
    hb = tile // HALO
    n_hb = n_tok // HALO
    kern = functools.partial(_ffn_kernel, tile=tile, n_tiles=n_tiles, ffn=ffn)
    return pl.pallas_call(
        kern,
        grid=(bsz, n_tiles),
        in_specs=[pl.BlockSpec((1, tile, d), lambda b, i: (b, i, 0)),
                  pl.BlockSpec((1, HALO, d), lambda b, i: (b, jnp.maximum(i * hb - 1, 0), 0)),
                  pl.BlockSpec((1, HALO, d), lambda b, i: (b, jnp.minimum((i + 1) * hb, n_hb - 1), 0)),
                  pl.BlockSpec((1, 6, d), lambda b, i: (b, 0, 0)),
                  _const_spec(ng.shape),
                  _const_spec(w_up.shape),
                  _const_spec(conv_w.shape),
                  _const_spec((1, ffn)),
                  _const_spec(w_down.shape)],
        out_specs=pl.BlockSpec((1, tile, d), lambda b, i: (b, i, 0)),
        out_shape=jax.ShapeDtypeStruct(x.shape, F32),
        scratch_shapes=[pltpu.VMEM((tile + 2 * HALO, d), BF16)],
        compiler_params=_params(2),
        name="conv_glu",
    )(x, x, x, mod, ng, w_up, conv_w, conv_b.reshape(1, ffn), w_down)


def _qkv_kernel(x_ref, mod_ref, ng_ref, w_ref, b_ref, cos_ref, sin_ref, q_ref, kt_ref, v_ref,
                *, q_dim, kv_dim):
    x = x_ref[0]
    mod = mod_ref[0]
    sh, sc = mod[0:1], mod[1:2]
    h = (_rms(x, ng_ref[0:1]) * (1.0 + sc) + sh).astype(BF16)
    half = h.shape[0] // 2
    qkv = jnp.concatenate([_dot(h[0:half], w_ref[...]), _dot(h[half:], w_ref[...])], axis=0) + b_ref[...]
    cosf = cos_ref[...]
    sinf = sin_ref[...]
    lane = lax.broadcasted_iota(jnp.int32, cosf.shape, 1)
    low_half = (lane % HEAD_DIM) < (HEAD_DIM // 2)
    scale = HEAD_DIM ** -0.5 * LOG2E
    for j in range((q_dim + kv_dim) // 128):
        blk = qkv[:, j * 128:(j + 1) * 128]
        partner = jnp.where(low_half, pltpu.roll(blk, 96, 1), pltpu.roll(blk, 32, 1))
        r = blk * cosf + partner * sinf
        if j * 128 < q_dim:
            q_ref[0, :, j * 128:(j + 1) * 128] = (r * scale).astype(BF16)
        else:
            kt_ref[0, j * 128 - q_dim:(j + 1) * 128 - q_dim, :] = r.T.astype(BF16)
    v_ref[0] = qkv[:, q_dim + kv_dim:].astype(BF16)


def _qkv(x, mod, ng, w_qkv, b_qkv, cosf, sinf, tile=512):
    bsz, n_tok, d = x.shape
    tile = min(tile, n_tok)
    q_dim = N_HEADS * HEAD_DIM
    kv_dim = N_KV_HEADS * HEAD_DIM
    n = q_dim + 2 * kv_dim
    kern = functools.partial(_qkv_kernel, q_dim=q_dim, kv_dim=kv_dim)
    return pl.pallas_call(
        kern,
        grid=(bsz, n_tok // tile),
        in_specs=[pl.BlockSpec((1, tile, d), lambda b, i: (b, i, 0)),
                  pl.BlockSpec((1, 6, d), lambda b, i: (b, 0, 0)),
                  _const_spec(ng.shape),
                  _const_spec(w_qkv.shape),
                  _const_spec((1, n)),
                  pl.BlockSpec((tile, 128), lambda b, i: (i, 0)),
                  pl.BlockSpec((tile, 128), lambda b, i: (i, 0))],
        out_specs=[pl.BlockSpec((1, tile, q_dim), lambda b, i: (b, i, 0)),
                   pl.BlockSpec((1, kv_dim, tile), lambda b, i: (b, 0, i)),
                   pl.BlockSpec((1, tile, kv_dim), lambda b, i: (b, i, 0))],
        out_shape=[jax.ShapeDtypeStruct((bsz, n_tok, q_dim), BF16),
                   jax.ShapeDtypeStruct((bsz, kv_dim, n_tok), BF16),
                   jax.ShapeDtypeStruct((bsz, n_tok, kv_dim), BF16)],
        compiler_params=_params(2),
        name="qkv_rope",
    )(x, mod, ng, w_qkv, b_qkv.reshape(1, n), cosf, sinf)


def _rope_tables(n_tok):
    rows = n_tok // GRID_W
    n_freq = HEAD_DIM // 4
    row = jnp.repeat(jnp.arange(rows), GRID_W).astype(F32)
    col = jnp.tile(jnp.arange(GRID_W), rows).astype(F32)
    inv_freq = ROPE_THETA ** (-(jnp.arange(n_freq, dtype=F32) / n_freq))
    ang = jnp.concatenate([row[:, None] * inv_freq, col[:, None] * inv_freq], axis=-1)
    cos, sin = jnp.cos(ang), jnp.sin(ang)
    cosf = jnp.tile(cos, (1, 4))
    sinf = jnp.tile(jnp.concatenate([-sin, sin], axis=-1), (1, 2))
    return cosf, sinf


def _attn_kernel(sink_ref, q_ref, ktp_ref, ktc_ref, ktn_ref, vp_ref, vc_ref, vn_ref, kctx_ref, vctx_ref,
                 x_ref, mod_ref, ng_ref, wo_ref, o_ref, kwin_ref, vwin_ref, attn_ref, bias_ref, s_ref, p_ref,
                 m_ref, es_ref, ve_ref, vo_ref, vce_ref, vco_ref,
                 *, tq, n_tiles):
    i = pl.program_id(1)
    nb = tq // BLOCK
    kwin_ref[:, 0:BLOCK] = ktp_ref[0]
    kwin_ref[:, BLOCK:BLOCK + tq] = ktc_ref[0]
    kwin_ref[:, BLOCK + tq:] = ktn_ref[0]
    vwin_ref[0:BLOCK] = vp_ref[0]
    vwin_ref[BLOCK:BLOCK + tq] = vc_ref[0]
    vwin_ref[BLOCK + tq:] = vn_ref[0]
    band_w = 3 * BLOCK
    r = lax.broadcasted_iota(jnp.int32, (BLOCK, band_w), 0)
    c = lax.broadcasted_iota(jnp.int32, (BLOCK, band_w), 1)
    band = (c >= r) & (c <= r + 2 * BLOCK)
    first = band & ((c >= BLOCK) | (i > 0))
    last = band & ((c < 2 * BLOCK) | (i < n_tiles - 1))
    neg = jnp.float32(-jnp.inf)
    bias_ref[0, 0] = jnp.where(first, 0.0, neg)[:, 0:BLOCK]
    bias_ref[0, 1] = jnp.where(band, 0.0, neg)[:, 0:BLOCK]
    bias_ref[1, 0] = jnp.where(last, 0.0, neg)[:, 2 * BLOCK:]
    bias_ref[1, 1] = jnp.where(band, 0.0, neg)[:, 2 * BLOCK:]

    lane = lax.broadcasted_iota(jnp.int32, (1, 2 * HEAD_DIM), 1)
    low = lane < HEAD_DIM
    one = jnp.ones((), BF16)

    def pad_values(src_ref, even_ref, odd_ref, n_rows):
        for t in range(N_KV_HEADS // 2):
            src = src_ref[0:n_rows, t * 128:(t + 1) * 128]
            swapped = pltpu.roll(src, HEAD_DIM, 1)
            even_ref[0:n_rows, (2 * t) * 128:(2 * t + 1) * 128] = jnp.where(low, src, one)
            odd_ref[0:n_rows, (2 * t) * 128:(2 * t + 1) * 128] = jnp.where(low, one, swapped)
            even_ref[0:n_rows, (2 * t + 1) * 128:(2 * t + 2) * 128] = jnp.where(low, swapped, one)
            odd_ref[0:n_rows, (2 * t + 1) * 128:(2 * t + 2) * 128] = jnp.where(low, one, src)

    pad_values(vwin_ref, ve_ref, vo_ref, tq + 2 * BLOCK)
    pad_values(vctx_ref.at[0], vce_ref, vco_ref, vctx_ref.shape[1])

    n_rows = Q_PER_KV * BLOCK
    half = n_rows // 2
    row_order = (0, 2, 1, 3)
    unit = 0
    for b in range(nb):
        bp = 0 if b == 0 else 1
        bn = 0 if b == nb - 1 else 1
        q = q_ref[0, b * BLOCK:(b + 1) * BLOCK, :]
        for h in range(N_KV_HEADS):
            par = unit % 2
            unit += 1
            heads = [h * Q_PER_KV + g for g in row_order]
            feat = slice(h * HEAD_DIM, (h + 1) * HEAD_DIM)
            tile = slice(h * 128, (h + 1) * 128)
            keys = slice(b * BLOCK, b * BLOCK + band_w)
            qh = jnp.concatenate([q[:, n * HEAD_DIM:(n + 1) * HEAD_DIM] for n in heads], axis=0)
            s_ref[par, :, 0:band_w] = _dot(qh, kwin_ref[feat, keys])
            s_ref[par, :, band_w:] = _dot(qh, kctx_ref[0, feat, :])
            for ci in range(n_rows // SM_ROWS):
                rows = slice(ci * SM_ROWS, (ci + 1) * SM_ROWS)
                r0 = (ci * SM_ROWS) % BLOCK
                sink = sink_ref[heads[ci * SM_ROWS // BLOCK]] * LOG2E
                mt = jnp.maximum(s_ref[par, rows, 0:BLOCK] + bias_ref[0, bp, r0:r0 + SM_ROWS, :],
                                 s_ref[par, rows, 2 * BLOCK:band_w] + bias_ref[1, bn, r0:r0 + SM_ROWS, :])
                for t in (1, 3, 4):
                    mt = jnp.maximum(mt, s_ref[par, rows, t * BLOCK:(t + 1) * BLOCK])
                m = jnp.maximum(jnp.max(mt, axis=-1, keepdims=True), sink)
                m_ref[par, rows, :] = jnp.broadcast_to(m, (SM_ROWS, BLOCK))
                es_ref[par, rows, :] = jnp.broadcast_to(jnp.exp2(sink - m), (SM_ROWS, BLOCK))
            for ci in range(n_rows // SM_ROWS):
                rows = slice(ci * SM_ROWS, (ci + 1) * SM_ROWS)
                r0 = (ci * SM_ROWS) % BLOCK
                m = m_ref[par, rows, :]
                for t in range(5):
                    st = s_ref[par, rows, t * BLOCK:(t + 1) * BLOCK]
                    if t == 0:
                        st = st + bias_ref[0, bp, r0:r0 + SM_ROWS, :]
                    if t == 2:
                        st = st + bias_ref[1, bn, r0:r0 + SM_ROWS, :]
                    p_ref[par, rows, t * BLOCK:(t + 1) * BLOCK] = jnp.exp2(st - m).astype(BF16)
            for part, (v_ref, vc_ref) in enumerate(((ve_ref, vce_ref), (vo_ref, vco_ref))):
                rows = slice(part * half, (part + 1) * half)
                acc = (_dot(p_ref[par, rows, 0:band_w], v_ref[keys, tile])
                       + _dot(p_ref[par, rows, band_w:], vc_ref[:, tile]))
                denom = pltpu.roll(acc, HEAD_DIM, 1) + es_ref[par, rows, :]
                o_ref_half = acc / denom
                if part == 0:
                    o_even = o_ref_half
                else:
                    o_odd = o_ref_half
            for gp in range(Q_PER_KV // 2):
                pair = jnp.where(low, o_even[gp * BLOCK:(gp + 1) * BLOCK], o_odd[gp * BLOCK:(gp + 1) * BLOCK])
                col = (h * Q_PER_KV + 2 * gp) * HEAD_DIM
                attn_ref[b * BLOCK:(b + 1) * BLOCK, col:col + 2 * HEAD_DIM] = pair.astype(BF16)
    y = _dot(attn_ref[...], wo_ref[...])
    gate = mod_ref[0][2:3]
    o_ref[0] = x_ref[0] + gate * _rms(y, ng_ref[1:2])


def _attn(x, q, kt, v, kctx_t, vctx, sink, mod, ng, w_o, tq=512):
    bsz, n_tok, d = x.shape
    n_tiles = n_tok // tq
    nb = tq // BLOCK
    n_blk = n_tok // BLOCK
    n_ctx = vctx.shape[1]
    q_dim = q.shape[2]
    kv_dim = v.shape[2]
    kern = functools.partial(_attn_kernel, tq=tq, n_tiles=n_tiles)
    cur_map = lambda b, i: (b, i, 0)
    batch_map = lambda b, i: (b, 0, 0)
    return pl.pallas_call(
        kern,
        grid=(bsz, n_tiles),
        in_specs=[pl.BlockSpec(memory_space=pltpu.SMEM),
                  pl.BlockSpec((1, tq, q_dim), cur_map),
                  pl.BlockSpec((1, kv_dim, BLOCK), lambda b, i: (b, 0, jnp.maximum(i * nb - 1, 0))),
                  pl.BlockSpec((1, kv_dim, tq), lambda b, i: (b, 0, i)),
                  pl.BlockSpec((1, kv_dim, BLOCK), lambda b, i: (b, 0, jnp.minimum((i + 1) * nb, n_blk - 1))),
                  pl.BlockSpec((1, BLOCK, kv_dim), lambda b, i: (b, jnp.maximum(i * nb - 1, 0), 0)),
                  pl.BlockSpec((1, tq, kv_dim), cur_map),
                  pl.BlockSpec((1, BLOCK, kv_dim), lambda b, i: (b, jnp.minimum((i + 1) * nb, n_blk - 1), 0)),
                  pl.BlockSpec((1, kv_dim, n_ctx), batch_map),
                  pl.BlockSpec((1, n_ctx, kv_dim), batch_map),
                  pl.BlockSpec((1, tq, d), cur_map),
                  pl.BlockSpec((1, 6, d), batch_map),
                  _const_spec(ng.shape),
                  _const_spec(w_o.shape)],
        out_specs=pl.BlockSpec((1, tq, d), cur_map),
        out_shape=jax.ShapeDtypeStruct(x.shape, F32),
        scratch_shapes=[pltpu.VMEM((kv_dim, tq + 2 * BLOCK), BF16),
                        pltpu.VMEM((tq + 2 * BLOCK, kv_dim), BF16),
                        pltpu.VMEM((tq, q_dim), BF16),
                        pltpu.VMEM((2, 2, BLOCK, BLOCK), F32),
                        pltpu.VMEM((2, Q_PER_KV * BLOCK, 3 * BLOCK + n_ctx), F32),
                        pltpu.VMEM((2, Q_PER_KV * BLOCK, 3 * BLOCK + n_ctx), BF16),
                        pltpu.VMEM((2, Q_PER_KV * BLOCK, BLOCK), F32),
                        pltpu.VMEM((2, Q_PER_KV * BLOCK, BLOCK), F32),
                        pltpu.VMEM((tq + 2 * BLOCK, N_KV_HEADS * 128), BF16),
                        pltpu.VMEM((tq + 2 * BLOCK, N_KV_HEADS * 128), BF16),
                        pltpu.VMEM((n_ctx, N_KV_HEADS * 128), BF16),
                        pltpu.VMEM((n_ctx, N_KV_HEADS * 128), BF16)],
        compiler_params=_params(2),
        name="window_attn",
    )(sink, q, kt, kt, kt, v, v, v, kctx_t, vctx, x, mod, ng, w_o)


def kernel(x, c, ctx, c_ctx, ada_w, ada_b, norm_g, a_w_in, a_ln_g, a_ln_b, a_w_s, a_b_s, a_w_out,
           b_w_qkv, b_b_qkv, b_sink, b_w_o, f_w_up, f_conv_w, f_conv_b, f_w_down):
    bsz, n_tok, d = x.shape
    n_ctx = ctx.shape[1]
    depth = ada_w.shape[0]
    assert depth == 2, "layer 0 = gMLP mixer, layer 1 = windowed attention (last layer: no context update)"

    rows = 8 * ((bsz + 1 + 7) // 8)
    cond = jnp.zeros((rows, d), F32).at[:bsz].set(c).at[bsz].set(c_ctx)
    mods = _ada(cond, ada_w, ada_b)

    def lat_mod(i):
        return mods[i, :bsz].reshape(bsz, 6, d)

    def ctx_mod(i):
        return jnp.broadcast_to(mods[i, bsz].reshape(1, 6, d), (bsz, 6, d))

    bf = lambda w: w.astype(BF16)

    gm = (norm_g[0], bf(a_w_in[0]), a_ln_g[0], a_ln_b[0], bf(a_w_s[0]), a_b_s[0], bf(a_w_out[0]))
    ff0 = (norm_g[0], bf(f_w_up[0]), f_conv_w[0], f_conv_b[0], bf(f_w_down[0]))
    x = _gmlp(x, lat_mod(0), *gm)
    ctx = _gmlp(ctx, ctx_mod(0), *gm)
    x = _ffn(x, lat_mod(0), *ff0)
    ctx = _ffn(ctx, ctx_mod(0), *ff0)

    cosf, sinf = _rope_tables(n_tok)
    w_qkv = bf(b_w_qkv[0])
    q, k, v = _qkv(x, lat_mod(1), norm_g[1], w_qkv, b_b_qkv[0], cosf, sinf)
    _, kctx, vctx = _qkv(ctx, ctx_mod(1), norm_g[1], w_qkv, b_b_qkv[0],
                         jnp.ones((n_ctx, 128), F32), jnp.zeros((n_ctx, 128), F32))
    x = _attn(x, q, k, v, kctx, vctx, b_sink[0], lat_mod(1), norm_g[1], bf(b_w_o[0]))
    x = _ffn(x, lat_mod(1), norm_g[1], bf(f_w_up[1]), f_conv_w[1], f_conv_b[1], bf(f_w_down[1]))
    return x
```

```python
import functools
import math

import jax
import jax.numpy as jnp
from jax import lax
from jax.experimental import pallas as pl
from jax.experimental.pallas import tpu as pltpu

EPS = 1e-6
GRID_W = 64
CHUNK = 128
A_GROUPS = 8
HEAD_DIM = 64
N_HEADS = 16
N_KV_HEADS = 4
Q_PER_KV = N_HEADS // N_KV_HEADS
BLOCK = 128
ROPE_THETA = 10000.0
LOG2E = math.log2(math.e)
SM_ROWS = 32
GMLP_CHUNK = 512
GMLP_OUT_GROUPS = 8
GMLP_SPLIT = 2
FFN_CHUNK = 2816
FFN_SPLIT = 2
HALO = 16
VMEM_LIMIT = 56 * 1024 * 1024

F32 = jnp.float32
BF16 = jnp.bfloat16


def _rms(x, g):
    return x * lax.rsqrt(jnp.mean(x * x, axis=-1, keepdims=True) + EPS) * g


def _gelu(x):
    c = math.sqrt(2.0 / math.pi)
    return x * (0.5 * (1.0 + jnp.tanh(c * (x + 0.044715 * (x * x * x)))))


def _dot(a, b):
    return jnp.dot(a, b, preferred_element_type=F32)


def _params(n_grid):
    return pltpu.CompilerParams(dimension_semantics=("parallel",) * n_grid,
                                vmem_limit_bytes=VMEM_LIMIT)


def _const_spec(shape):
    nd = len(shape)
    return pl.BlockSpec(shape, lambda *_: (0,) * nd, pipeline_mode=pl.Buffered(1))


def _ada_kernel(s_ref, w_ref, b_ref, o_ref):
    s = s_ref[...]
    s = s * jax.nn.sigmoid(s)
    o_ref[0] = _dot(s.astype(BF16), w_ref[0].astype(BF16)) + b_ref[0]


def _ada(cond, ada_w, ada_b, tn=1536):
    depth, d, n = ada_w.shape
    r = cond.shape[0]
    return pl.pallas_call(
        _ada_kernel,
        grid=(depth, n // tn),
        in_specs=[pl.BlockSpec((r, d), lambda i, j: (0, 0)),
                  pl.BlockSpec((1, d, tn), lambda i, j: (i, 0, j)),
                  pl.BlockSpec((1, 1, tn), lambda i, j: (i, 0, j))],
        out_specs=pl.BlockSpec((1, r, tn), lambda i, j: (i, 0, j)),
        out_shape=jax.ShapeDtypeStruct((depth, r, n), F32),
        compiler_params=_params(2),
        name="ada_mod",
    )(cond, ada_w, ada_b.reshape(depth, 1, n))


def _gmlp_kernel(x_ref, mod_ref, ng_ref, w_in_ref, lng_ref, lnb_ref, ws_ref, bs_ref, w_out_ref,
                 o_ref, h_ref, v_ref, t_ref, *, tile, width):
    x = x_ref[0]
    mod = mod_ref[0]
    sh, sc, gate = mod[0:1], mod[1:2], mod[2:3]
    h_ref[...] = (_rms(x, ng_ref[0:1]) * (1.0 + sc) + sh).astype(BF16)
    half = max(tile // GMLP_SPLIT, CHUNK)

    def dot_rows(lhs_ref, w):
        return jnp.concatenate([_dot(lhs_ref[r0:r0 + half], w) for r0 in range(0, tile, half)], axis=0)

    vsum = jnp.zeros((tile, 1), F32)
    for c0 in range(0, width, GMLP_CHUNK):
        vj = _gelu(dot_rows(h_ref, w_in_ref[:, width + c0:width + c0 + GMLP_CHUNK]))
        v_ref[:, c0:c0 + GMLP_CHUNK] = vj
        vsum = vsum + jnp.sum(vj, axis=-1, keepdims=True)
    mu = vsum * (1.0 / width)
    vss = jnp.zeros((tile, 1), F32)
    for c0 in range(0, width, GMLP_CHUNK):
        dv = v_ref[:, c0:c0 + GMLP_CHUNK] - mu
        vss = vss + jnp.sum(dv * dv, axis=-1, keepdims=True)
    rstd = lax.rsqrt(vss * (1.0 / width) + EPS)
    gw = width // A_GROUPS
    y = None
    for g in range(A_GROUPS):
        cols = slice(g * gw, (g + 1) * gw)
        vn = ((v_ref[:, cols] - mu) * rstd * lng_ref[:, cols] + lnb_ref[:, cols]).astype(BF16)
        u = _gelu(dot_rows(h_ref, w_in_ref[:, cols]))
        s = jnp.concatenate([_dot(ws_ref[g], vn[r0:r0 + CHUNK]) + bs_ref[g]
                             for r0 in range(0, tile, CHUNK)], axis=0)
        t_ref[:, cols] = (u * s).astype(BF16)
        if (g + 1) % GMLP_OUT_GROUPS == 0:
            k0 = (g + 1 - GMLP_OUT_GROUPS) * gw
            part = dot_rows(t_ref.at[:, k0:(g + 1) * gw], w_out_ref[k0:(g + 1) * gw, :])
            y = part if y is None else y + part
    o_ref[0] = x + gate * _rms(y, ng_ref[1:2])


def _gmlp(x, mod, ng, w_in, ln_g, ln_b, w_s, b_s, w_out, tile=512):
    bsz, n_tok, d = x.shape
    width = w_out.shape[0]
    gw = width // A_GROUPS
    tile = min(tile, n_tok)
    bs_full = jnp.broadcast_to(b_s[:, :, None], (A_GROUPS, CHUNK, gw))
    kern = functools.partial(_gmlp_kernel, tile=tile, width=width)
    return pl.pallas_call(
        kern,
        grid=(bsz, n_tok // tile),
        in_specs=[pl.BlockSpec((1, tile, d), lambda b, i: (b, i, 0)),
                  pl.BlockSpec((1, 6, d), lambda b, i: (b, 0, 0)),
                  _const_spec(ng.shape),
                  _const_spec(w_in.shape),
                  _const_spec((1, width)),
                  _const_spec((1, width)),
                  _const_spec(w_s.shape),
                  _const_spec(bs_full.shape),
                  _const_spec(w_out.shape)],
        out_specs=pl.BlockSpec((1, tile, d), lambda b, i: (b, i, 0)),
        out_shape=jax.ShapeDtypeStruct(x.shape, F32),
        scratch_shapes=[pltpu.VMEM((tile, d), BF16),
                        pltpu.VMEM((tile, width), F32),
                        pltpu.VMEM((tile, width), BF16)],
        compiler_params=_params(2),
        name="gmlp_mixer",
    )(x, mod, ng, w_in, ln_g.reshape(1, width), ln_b.reshape(1, width), w_s, bs_full, w_out)


def _ffn_kernel(x_ref, xp_ref, xn_ref, mod_ref, ng_ref, w_up_ref, cw_ref, cb_ref, w_dn_ref,
                o_ref, h_ref, *, tile, n_tiles, ffn):
    i = pl.program_id(1)
    mod = mod_ref[0]
    sh, sc, gate = mod[3:4], mod[4:5], mod[5:6]
    g_pre = ng_ref[2:3]

    def pre(xx):
        return _rms(xx, g_pre) * (1.0 + sc) + sh

    x = x_ref[0]
    has_prev = (i > 0).astype(F32)
    has_next = (i < n_tiles - 1).astype(F32)
    h_ref[0:HALO] = (pre(xp_ref[0]) * has_prev).astype(BF16)
    h_ref[HALO:HALO + tile] = pre(x).astype(BF16)
    h_ref[HALO + tile:] = (pre(xn_ref[0]) * has_next).astype(BF16)
    ext = tile + 2 * HALO
    y = None
    c0 = 0
    while c0 < ffn:
        cw = min(FFN_CHUNK, ffn - c0)
        cols = slice(c0, c0 + cw)
        cuts = [0] + [HALO + (tile * k) // FFN_SPLIT for k in range(1, FFN_SPLIT)] + [ext]
        inner = [HALO] + cuts[1:-1] + [HALO + tile]
        a_ext = jnp.concatenate([_dot(h_ref[r0:r1], w_up_ref[:, cols])
                                 for r0, r1 in zip(cuts[:-1], cuts[1:])], axis=0)
        a = (pltpu.roll(a_ext, 1, 0)[HALO:HALO + tile] * cw_ref[0:1, cols]
             + a_ext[HALO:HALO + tile] * cw_ref[1:2, cols]
             + pltpu.roll(a_ext, ext - 1, 0)[HALO:HALO + tile] * cw_ref[2:3, cols]
             + cb_ref[:, cols])
        w_b = w_up_ref[:, ffn + c0:ffn + c0 + cw]
        b = jnp.concatenate([_dot(h_ref[r0:r1], w_b) for r0, r1 in zip(inner[:-1], inner[1:])], axis=0)
        g = (_gelu(a) * b).astype(BF16)
        part = jnp.concatenate([_dot(g[r0 - HALO:r1 - HALO], w_dn_ref[cols, :])
                                for r0, r1 in zip(inner[:-1], inner[1:])], axis=0)
        y = part if y is None else y + part
        c0 += cw
    o_ref[0] = x + gate * _rms(y, ng_ref[3:4])


def _ffn(x, mod, ng, w_up, conv_w, conv_b, w_down, tile=512):
    bsz, n_tok, d = x.shape
    ffn = w_down.shape[0]
    tile = min(tile, n_tok)
    n_tiles = n_tok // tile
    hb = tile // HALO
    n_hb = n_tok // HALO
    kern = functools.partial(_ffn_kernel, tile=tile, n_tiles=n_tiles, ffn=ffn)
    return pl.pallas_call(
        kern,
        grid=(bsz, n_tiles),
        in_specs=[pl.BlockSpec((1, tile, d), lambda b, i: (b, i, 0)),
                  pl.BlockSpec((1, HALO, d), lambda b, i: (b, jnp.maximum(i * hb - 1, 0), 0)),
                  pl.BlockSpec((1, HALO, d), lambda b, i: (b, jnp.minimum((i + 1) * hb, n_hb - 1), 0)),
                  pl.BlockSpec((1, 6, d), lambda b, i: (b, 0, 0)),
                  _const_spec(ng.shape),
                  _const_spec(w_up.shape),
                  _const_spec(conv_w.shape),
                  _const_spec((1, ffn)),
                  _const_spec(w_down.shape)],
        out_specs=pl.BlockSpec((1, tile, d), lambda b, i: (b, i, 0)),
        out_shape=jax.ShapeDtypeStruct(x.shape, F32),
        scratch_shapes=[pltpu.VMEM((tile + 2 * HALO, d), BF16)],
        compiler_params=_params(2),
        name="conv_glu",
    )(x, x, x, mod, ng, w_up, conv_w, conv_b.reshape(1, ffn), w_down)


def _qkv_kernel(x_ref, mod_ref, ng_ref, w_ref, b_ref, cos_ref, sin_ref, q_ref, kt_ref, v_ref,
                *, q_dim, kv_dim):
    x = x_ref[0]
    mod = mod_ref[0]
    sh, sc = mod[0:1], mod[1:2]
    h = (_rms(x, ng_ref[0:1]) * (1.0 + sc) + sh).astype(BF16)
    half = h.shape[0] // 2
    qkv = jnp.concatenate([_dot(h[0:half], w_ref[...]), _dot(h[half:], w_ref[...])], axis=0) + b_ref[...]
    cosf = cos_ref[...]
    sinf = sin_ref[...]
    lane = lax.broadcasted_iota(jnp.int32, cosf.shape, 1)
    low_half = (lane % HEAD_DIM) < (HEAD_DIM // 2)
    scale = HEAD_DIM ** -0.5 * LOG2E
    for j in range((q_dim + kv_dim) // 128):
        blk = qkv[:, j * 128:(j + 1) * 128]
        partner = jnp.where(low_half, pltpu.roll(blk, 96, 1), pltpu.roll(blk, 32, 1))
        r = blk * cosf + partner * sinf
        if j * 128 < q_dim:
            q_ref[0, :, j * 128:(j + 1) * 128] = (r * scale).astype(BF16)
        else:
            kt_ref[0, j * 128 - q_dim:(j + 1) * 128 - q_dim, :] = r.T.astype(BF16)
    v_ref[0] = qkv[:, q_dim + kv_dim:].astype(BF16)


def _qkv(x, mod, ng, w_qkv, b_qkv, cosf, sinf, tile=512):
    bsz, n_tok, d = x.shape
    tile = min(tile, n_tok)
    q_dim = N_HEADS * HEAD_DIM
    kv_dim = N_KV_HEADS * HEAD_DIM
    n = q_dim + 2 * kv_dim
    kern = functools.partial(_qkv_kernel, q_dim=q_dim, kv_dim=kv_dim)
    return pl.pallas_call(
        kern,
        grid=(bsz, n_tok // tile),
        in_specs=[pl.BlockSpec((1, tile, d), lambda b, i: (b, i, 0)),
                  pl.BlockSpec((1, 6, d), lambda b, i: (b, 0, 0)),
                  _const_spec(ng.shape),
                  _const_spec(w_qkv.shape),
                  _const_spec((1, n)),
                  pl.BlockSpec((tile, 128), lambda b, i: (i, 0)),
                  pl.BlockSpec((tile, 128), lambda b, i: (i, 0))],
        out_specs=[pl.BlockSpec((1, tile, q_dim), lambda b, i: (b, i, 0)),
                   pl.BlockSpec((1, kv_dim, tile), lambda b, i: (b, 0, i)),
                   pl.BlockSpec((1, tile, kv_dim), lambda b, i: (b, i, 0))],
        out_shape=[jax.ShapeDtypeStruct((bsz, n_tok, q_dim), BF16),
                   jax.ShapeDtypeStruct((bsz, kv_dim, n_tok), BF16),
                   jax.ShapeDtypeStruct((bsz, n_tok, kv_dim), BF16)],
        compiler_params=_params(2),
        name="qkv_rope",
    )(x, mod, ng, w_qkv, b_qkv.reshape(1, n), cosf, sinf)


def _rope_tables(n_tok):
    rows = n_tok // GRID_W
    n_freq = HEAD_DIM // 4
    row = jnp.repeat(jnp.arange(rows), GRID_W).astype(F32)
    col = jnp.tile(jnp.arange(GRID_W), rows).astype(F32)
    inv_freq = ROPE_THETA ** (-(jnp.arange(n_freq, dtype=F32) / n_freq))
    ang = jnp.concatenate([row[:, None] * inv_freq, col[:, None] * inv_freq], axis=-1)
    cos, sin = jnp.cos(ang), jnp.sin(ang)
    cosf = jnp.tile(cos, (1, 4))
    sinf = jnp.tile(jnp.concatenate([-sin, sin], axis=-1), (1, 2))
    return cosf, sinf


def _attn_kernel(sink_ref, q_ref, ktp_ref, ktc_ref, ktn_ref, vp_ref, vc_ref, vn_ref, kctx_ref, vctx_ref,
                 x_ref, mod_ref, ng_ref, wo_ref, o_ref, kwin_ref, vwin_ref, attn_ref, bias_ref, s_ref, p_ref,
                 m_ref, es_ref, ve_ref, vo_ref, vce_ref, vco_ref,
                 *, tq, n_tiles):
    i = pl.program_id(1)
    nb = tq // BLOCK
    kwin_ref[:, 0:BLOCK] = ktp_ref[0]
    kwin_ref[:, BLOCK:BLOCK + tq] = ktc_ref[0]
    kwin_ref[:, BLOCK + tq:] = ktn_ref[0]
    vwin_ref[0:BLOCK] = vp_ref[0]
    vwin_ref[BLOCK:BLOCK + tq] = vc_ref[0]
    vwin_ref[BLOCK + tq:] = vn_ref[0]
    band_w = 3 * BLOCK
    r = lax.broadcasted_iota(jnp.int32, (BLOCK, band_w), 0)
    c = lax.broadcasted_iota(jnp.int32, (BLOCK, band_w), 1)
    band = (c >= r) & (c <= r + 2 * BLOCK)
    first = band & ((c >= BLOCK) | (i > 0))
    last = band & ((c < 2 * BLOCK) | (i < n_tiles - 1))
    neg = jnp.float32(-jnp.inf)
    bias_ref[0, 0] = jnp.where(first, 0.0, neg)[:, 0:BLOCK]
    bias_ref[0, 1] = jnp.where(band, 0.0, neg)[:, 0:BLOCK]
    bias_ref[1, 0] = jnp.where(last, 0.0, neg)[:, 2 * BLOCK:]
    bias_ref[1, 1] = jnp.where(band, 0.0, neg)[:, 2 * BLOCK:]

    lane = lax.broadcasted_iota(jnp.int32, (1, 2 * HEAD_DIM), 1)
    low = lane < HEAD_DIM
    one = jnp.ones((), BF16)

    def pad_values(src_ref, even_ref, odd_ref, n_rows):
        for t in range(N_KV_HEADS // 2):
            src = src_ref[0:n_rows, t * 128:(t + 1) * 128]
            swapped = pltpu.roll(src, HEAD_DIM, 1)
            even_ref[0:n_rows, (2 * t) * 128:(2 * t + 1) * 128] = jnp.where(low, src, one)
            odd_ref[0:n_rows, (2 * t) * 128:(2 * t + 1) * 128] = jnp.where(low, one, swapped)
            even_ref[0:n_rows, (2 * t + 1) * 128:(2 * t + 2) * 128] = jnp.where(low, swapped, one)
            odd_ref[0:n_rows, (2 * t + 1) * 128:(2 * t + 2) * 128] = jnp.where(low, one, src)

    pad_values(vwin_ref, ve_ref, vo_ref, tq + 2 * BLOCK)
    pad_values(vctx_ref.at[0], vce_ref, vco_ref, vctx_ref.shape[1])

    n_rows = Q_PER_KV * BLOCK
    half = n_rows // 2
    row_order = (0, 2, 1, 3)
    unit = 0
    for b in range(nb):
        bp = 0 if b == 0 else 1
        bn = 0 if b == nb - 1 else 1
        q = q_ref[0, b * BLOCK:(b + 1) * BLOCK, :]
        for h in range(N_KV_HEADS):
            par = unit % 2
            unit += 1
            heads = [h * Q_PER_KV + g for g in row_order]
            feat = slice(h * HEAD_DIM, (h + 1) * HEAD_DIM)
            tile = slice(h * 128, (h + 1) * 128)
            keys = slice(b * BLOCK, b * BLOCK + band_w)
            qh = jnp.concatenate([q[:, n * HEAD_DIM:(n + 1) * HEAD_DIM] for n in heads], axis=0)
            s_ref[par, :, 0:band_w] = _dot(qh, kwin_ref[feat, keys])
            s_ref[par, :, band_w:] = _dot(qh, kctx_ref[0, feat, :])
            for ci in range(n_rows // SM_ROWS):
                rows = slice(ci * SM_ROWS, (ci + 1) * SM_ROWS)
                r0 = (ci * SM_ROWS) % BLOCK
                sink = sink_ref[heads[ci * SM_ROWS // BLOCK]] * LOG2E
                mt = jnp.maximum(s_ref[par, rows, 0:BLOCK] + bias_ref[0, bp, r0:r0 + SM_ROWS, :],
                                 s_ref[par, rows, 2 * BLOCK:band_w] + bias_ref[1, bn, r0:r0 + SM_ROWS, :])
                for t in (1, 3, 4):
                    mt = jnp.maximum(mt, s_ref[par, rows, t * BLOCK:(t + 1) * BLOCK])
                m = jnp.maximum(jnp.max(mt, axis=-1, keepdims=True), sink)
                m_ref[par, rows, :] = jnp.broadcast_to(m, (SM_ROWS, BLOCK))
                es_ref[par, rows, :] = jnp.broadcast_to(jnp.exp2(sink - m), (SM_ROWS, BLOCK))
            for ci in range(n_rows // SM_ROWS):
                rows = slice(ci * SM_ROWS, (ci + 1) * SM_ROWS)
                r0 = (ci * SM_ROWS) % BLOCK
                m = m_ref[par, rows, :]
                for t in range(5):
                    st = s_ref[par, rows, t * BLOCK:(t + 1) * BLOCK]
                    if t == 0:
                        st = st + bias_ref[0, bp, r0:r0 + SM_ROWS, :]
                    if t == 2:
                        st = st + bias_ref[1, bn, r0:r0 + SM_ROWS, :]
                    p_ref[par, rows, t * BLOCK:(t + 1) * BLOCK] = jnp.exp2(st - m).astype(BF16)
            for part, (v_ref, vc_ref) in enumerate(((ve_ref, vce_ref), (vo_ref, vco_ref))):
                rows = slice(part * half, (part + 1) * half)
                acc = (_dot(p_ref[par, rows, 0:band_w], v_ref[keys, tile])
                       + _dot(p_ref[par, rows, band_w:], vc_ref[:, tile]))
                denom = pltpu.roll(acc, HEAD_DIM, 1) + es_ref[par, rows, :]
                o_ref_half = acc / denom
                if part == 0:
                    o_even = o_ref_half
                else:
                    o_odd = o_ref_half
            for gp in range(Q_PER_KV // 2):
                pair = jnp.where(low, o_even[gp * BLOCK:(gp + 1) * BLOCK], o_odd[gp * BLOCK:(gp + 1) * BLOCK])
                col = (h * Q_PER_KV + 2 * gp) * HEAD_DIM
                attn_ref[b * BLOCK:(b + 1) * BLOCK, col:col + 2 * HEAD_DIM] = pair.astype(BF16)
    y = _dot(attn_ref[...], wo_ref[...])
    gate = mod_ref[0][2:3]
    o_ref[0] = x_ref[0] + gate * _rms(y, ng_ref[1:2])


def _attn(x, q, kt, v, kctx_t, vctx, sink, mod, ng, w_o, tq=512):
    bsz, n_tok, d = x.shape
    n_tiles = n_tok // tq
    nb = tq // BLOCK
    n_blk = n_tok // BLOCK
    n_ctx = vctx.shape[1]
    q_dim = q.shape[2]
    kv_dim = v.shape[2]
    kern = functools.partial(_attn_kernel, tq=tq, n_tiles=n_tiles)
    cur_map = lambda b, i: (b, i, 0)
    batch_map = lambda b, i: (b, 0, 0)
    return pl.pallas_call(
        kern,
        grid=(bsz, n_tiles),
        in_specs=[pl.BlockSpec(memory_space=pltpu.SMEM),
                  pl.BlockSpec((1, tq, q_dim), cur_map),
                  pl.BlockSpec((1, kv_dim, BLOCK), lambda b, i: (b, 0, jnp.maximum(i * nb - 1, 0))),
                  pl.BlockSpec((1, kv_dim, tq), lambda b, i: (b, 0, i)),
                  pl.BlockSpec((1, kv_dim, BLOCK), lambda b, i: (b, 0, jnp.minimum((i + 1) * nb, n_blk - 1))),
                  pl.BlockSpec((1, BLOCK, kv_dim), lambda b, i: (b, jnp.maximum(i * nb - 1, 0), 0)),
                  pl.BlockSpec((1, tq, kv_dim), cur_map),
                  pl.BlockSpec((1, BLOCK, kv_dim), lambda b, i: (b, jnp.minimum((i + 1) * nb, n_blk - 1), 0)),
                  pl.BlockSpec((1, kv_dim, n_ctx), batch_map),
                  pl.BlockSpec((1, n_ctx, kv_dim), batch_map),
                  pl.BlockSpec((1, tq, d), cur_map),
                  pl.BlockSpec((1, 6, d), batch_map),
                  _const_spec(ng.shape),
                  _const_spec(w_o.shape)],
        out_specs=pl.BlockSpec((1, tq, d), cur_map),
        out_shape=jax.ShapeDtypeStruct(x.shape, F32),
        scratch_shapes=[pltpu.VMEM((kv_dim, tq + 2 * BLOCK), BF16),
                        pltpu.VMEM((tq + 2 * BLOCK, kv_dim), BF16),
                        pltpu.VMEM((tq, q_dim), BF16),
                        pltpu.VMEM((2, 2, BLOCK, BLOCK), F32),
                        pltpu.VMEM((2, Q_PER_KV * BLOCK, 3 * BLOCK + n_ctx), F32),
                        pltpu.VMEM((2, Q_PER_KV * BLOCK, 3 * BLOCK + n_ctx), BF16),
                        pltpu.VMEM((2, Q_PER_KV * BLOCK, BLOCK), F32),
                        pltpu.VMEM((2, Q_PER_KV * BLOCK, BLOCK), F32),
                        pltpu.VMEM((tq + 2 * BLOCK, N_KV_HEADS * 128), BF16),
                        pltpu.VMEM((tq + 2 * BLOCK, N_KV_HEADS * 128), BF16),
                        pltpu.VMEM((n_ctx, N_KV_HEADS * 128), BF16),
                        pltpu.VMEM((n_ctx, N_KV_HEADS * 128), BF16)],
        compiler_params=_params(2),
        name="window_attn",
    )(sink, q, kt, kt, kt, v, v, v, kctx_t, vctx, x, mod, ng, w_o)


def kernel(x, c, ctx, c_ctx, ada_w, ada_b, norm_g, a_w_in, a_ln_g, a_ln_b, a_w_s, a_b_s, a_w_out,
           b_w_qkv, b_b_qkv, b_sink, b_w_o, f_w_up, f_conv_w, f_conv_b, f_w_down):
    bsz, n_tok, d = x.shape
    n_ctx = ctx.shape[1]
    depth = ada_w.shape[0]
    assert depth == 2, "layer 0 = gMLP mixer, layer 1 = windowed attention (last layer: no context update)"

    rows = 8 * ((bsz + 1 + 7) // 8)
    cond = jnp.zeros((rows, d), F32).at[:bsz].set(c).at[bsz].set(c_ctx)
    mods = _ada(cond, ada_w, ada_b)

    def lat_mod(i):
        return mods[i, :bsz].reshape(bsz, 6, d)

    def ctx_mod(i):
        return jnp.broadcast_to(mods[i, bsz].reshape(1, 6, d), (bsz, 6, d))

    bf = lambda w: w.astype(BF16)

    gm = (norm_g[0], bf(a_w_in[0]), a_ln_g[0], a_ln_b[0], bf(a_w_s[0]), a_b_s[0], bf(a_w_out[0]))
    ff0 = (norm_g[0], bf(f_w_up[0]), f_conv_w[0], f_conv_b[0], bf(f_w_down[0]))
    x = _gmlp(x, lat_mod(0), *gm)
    ctx = _gmlp(ctx, ctx_mod(0), *gm)
    x = _ffn(x, lat_mod(0), *ff0)
    ctx = _ffn(ctx, ctx_mod(0), *ff0)

    cosf, sinf = _rope_tables(n_tok)
    w_qkv = bf(b_w_qkv[0])
    q, k, v = _qkv(x, lat_mod(1), norm_g[1], w_qkv, b_b_qkv[0], cosf, sinf)
    _, kctx, vctx = _qkv(ctx, ctx_mod(1), norm_g[1], w_qkv, b_b_qkv[0],
                         jnp.ones((n_ctx, 128), F32), jnp.zeros((n_ctx, 128), F32))
    x = _attn(x, q, k, v, kctx, vctx, b_sink[0], lat_mod(1), norm_g[1], bf(b_w_o[0]))
    x = _ffn(x, lat_mod(1), norm_g[1], bf(f_w_up[1]), f_conv_w[1], f_conv_b[1], bf(f_w_down[1]))
    return x
```

```python
import functools
import math

import jax
import jax.numpy as jnp
from jax import lax
from jax.experimental import pallas as pl
from jax.experimental.pallas import tpu as pltpu

EPS = 1e-6
GRID_W = 64
CHUNK = 128
A_GROUPS = 8
HEAD_DIM = 64
N_HEADS = 16
N_KV_HEADS = 4
Q_PER_KV = N_HEADS // N_KV_HEADS
BLOCK = 128
ROPE_THETA = 10000.0
LOG2E = math.log2(math.e)
SM_ROWS = 32
GMLP_CHUNK = 512
GMLP_OUT_GROUPS = 8
GMLP_SPLIT = 2
FFN_CHUNK = 2816
FFN_SPLIT = 2
HALO = 16
VMEM_LIMIT = 56 * 1024 * 1024

F32 = jnp.float32
BF16 = jnp.bfloat16


def _rms(x, g):
    return x * lax.rsqrt(jnp.mean(x * x, axis=-1, keepdims=True) + EPS) * g


def _gelu(x):
    c = math.sqrt(2.0 / math.pi)
    return x * (0.5 * (1.0 + jnp.tanh(c * (x + 0.044715 * (x * x * x)))))


def _dot(a, b):
    return jnp.dot(a, b, preferred_element_type=F32)


def _params(n_grid):
    return pltpu.CompilerParams(dimension_semantics=("parallel",) * n_grid,
                                vmem_limit_bytes=VMEM_LIMIT)


def _const_spec(shape):
    nd = len(shape)
    return pl.BlockSpec(shape, lambda *_: (0,) * nd, pipeline_mode=pl.Buffered(1))


def _ada_kernel(s_ref, w_ref, b_ref, o_ref):
    s = s_ref[...]
    s = s * jax.nn.sigmoid(s)
    o_ref[0] = _dot(s.astype(BF16), w_ref[0].astype(BF16)) + b_ref[0]


def _ada(cond, ada_w, ada_b, tn=1536):
    depth, d, n = ada_w.shape
    r = cond.shape[0]
    return pl.pallas_call(
        _ada_kernel,
        grid=(depth, n // tn),
        in_specs=[pl.BlockSpec((r, d), lambda i, j: (0, 0)),
                  pl.BlockSpec((1, d, tn), lambda i, j: (i, 0, j)),
                  pl.BlockSpec((1, 1, tn), lambda i, j: (i, 0, j))],
        out_specs=pl.BlockSpec((1, r, tn), lambda i, j: (i, 0, j)),
        out_shape=jax.ShapeDtypeStruct((depth, r, n), F32),
        compiler_params=_params(2),
        name="ada_mod",
    )(cond, ada_w, ada_b.reshape(depth, 1, n))


def _gmlp_kernel(x_ref, mod_ref, ng_ref, w_in_ref, lng_ref, lnb_ref, ws_ref, bs_ref, w_out_ref,
                 o_ref, h_ref, v_ref, t_ref, *, tile, width):
    x = x_ref[0]
    mod = mod_ref[0]
    sh, sc, gate = mod[0:1], mod[1:2], mod[2:3]
    h_ref[...] = (_rms(x, ng_ref[0:1]) * (1.0 + sc) + sh).astype(BF16)
    half = max(tile // GMLP_SPLIT, CHUNK)

    def dot_rows(lhs_ref, w):
        return jnp.concatenate([_dot(lhs_ref[r0:r0 + half], w) for r0 in range(0, tile, half)], axis=0)

    vsum = jnp.zeros((tile, 1), F32)
    for c0 in range(0, width, GMLP_CHUNK):
        vj = _gelu(dot_rows(h_ref, w_in_ref[:, width + c0:width + c0 + GMLP_CHUNK]))
        v_ref[:, c0:c0 + GMLP_CHUNK] = vj
        vsum = vsum + jnp.sum(vj, axis=-1, keepdims=True)
    mu = vsum * (1.0 / width)
    vss = jnp.zeros((tile, 1), F32)
    for c0 in range(0, width, GMLP_CHUNK):
        dv = v_ref[:, c0:c0 + GMLP_CHUNK] - mu
        vss = vss + jnp.sum(dv * dv, axis=-1, keepdims=True)
    rstd = lax.rsqrt(vss * (1.0 / width) + EPS)
    gw = width // A_GROUPS
    y = None
    for g in range(A_GROUPS):
        cols = slice(g * gw, (g + 1) * gw)
        vn = ((v_ref[:, cols] - mu) * rstd * lng_ref[:, cols] + lnb_ref[:, cols]).astype(BF16)
        u = _gelu(dot_rows(h_ref, w_in_ref[:, cols]))
        s = jnp.concatenate([_dot(ws_ref[g], vn[r0:r0 + CHUNK]) + bs_ref[g]
                             for r0 in range(0, tile, CHUNK)], axis=0)
        t_ref[:, cols] = (u * s).astype(BF16)
        if (g + 1) % GMLP_OUT_GROUPS == 0:
            k0 = (g + 1 - GMLP_OUT_GROUPS) * gw
            part = dot_rows(t_ref.at[:, k0:(g + 1) * gw], w_out_ref[k0:(g + 1) * gw, :])
            y = part if y is None else y + part
    o_ref[0] = x + gate * _rms(y, ng_ref[1:2])


def _gmlp(x, mod, ng, w_in, ln_g, ln_b, w_s, b_s, w_out, tile=512):
    bsz, n_tok, d = x.shape
    width = w_out.shape[0]
    gw = width // A_GROUPS
    tile = min(tile, n_tok)
    bs_full = jnp.broadcast_to(b_s[:, :, None], (A_GROUPS, CHUNK, gw))
    kern = functools.partial(_gmlp_kernel, tile=tile, width=width)
    return pl.pallas_call(
        kern,
        grid=(bsz, n_tok // tile),
        in_specs=[pl.BlockSpec((1, tile, d), lambda b, i: (b, i, 0)),
                  pl.BlockSpec((1, 6, d), lambda b, i: (b, 0, 0)),
                  _const_spec(ng.shape),
                  _const_spec(w_in.shape),
                  _const_spec((1, width)),
                  _const_spec((1, width)),
                  _const_spec(w_s.shape),
                  _const_spec(bs_full.shape),
                  _const_spec(w_out.shape)],
        out_specs=pl.BlockSpec((1, tile, d), lambda b, i: (b, i, 0)),
        out_shape=jax.ShapeDtypeStruct(x.shape, F32),
        scratch_shapes=[pltpu.VMEM((tile, d), BF16),
                        pltpu.VMEM((tile, width), F32),
                        pltpu.VMEM((tile, width), BF16)],
        compiler_params=_params(2),
        name="gmlp_mixer",
    )(x, mod, ng, w_in, ln_g.reshape(1, width), ln_b.reshape(1, width), w_s, bs_full, w_out)


def _ffn_kernel(x_ref, xp_ref, xn_ref, mod_ref, ng_ref, w_up_ref, cw_ref, cb_ref, w_dn_ref,
                o_ref, h_ref, *, tile, n_tiles, ffn):
    i = pl.program_id(1)
    mod = mod_ref[0]
    sh, sc, gate = mod[3:4], mod[4:5], mod[5:6]
    g_pre = ng_ref[2:3]

    def pre(xx):
        return _rms(xx, g_pre) * (1.0 + sc) + sh

    x = x_ref[0]
    has_prev = (i > 0).astype(F32)
    has_next = (i < n_tiles - 1).astype(F32)
    h_ref[0:HALO] = (pre(xp_ref[0]) * has_prev).astype(BF16)
    h_ref[HALO:HALO + tile] = pre(x).astype(BF16)
    h_ref[HALO + tile:] = (pre(xn_ref[0]) * has_next).astype(BF16)
    ext = tile + 2 * HALO
    y = None
    c0 = 0
    while c0 < ffn:
        cw = min(FFN_CHUNK, ffn - c0)
        cols = slice(c0, c0 + cw)
        cuts = [0] + [HALO + (tile * k) // FFN_SPLIT for k in range(1, FFN_SPLIT)] + [ext]
        inner = [HALO] + cuts[1:-1] + [HALO + tile]
        a_ext = jnp.concatenate([_dot(h_ref[r0:r1], w_up_ref[:, cols])
                                 for r0, r1 in zip(cuts[:-1], cuts[1:])], axis=0)
        a = (pltpu.roll(a_ext, 1, 0)[HALO:HALO + tile] * cw_ref[0:1, cols]
             + a_ext[HALO:HALO + tile] * cw_ref[1:2, cols]
             + pltpu.roll(a_ext, ext - 1, 0)[HALO:HALO + tile] * cw_ref[2:3, cols]
             + cb_ref[:, cols])
        w_b = w_up_ref[:, ffn + c0:ffn + c0 + cw]
        b = jnp.concatenate([_dot(h_ref[r0:r1], w_b) for r0, r1 in zip(inner[:-1], inner[1:])], axis=0)
        g = (_gelu(a) * b).astype(BF16)
        part = jnp.concatenate([_dot(g[r0 - HALO:r1 - HALO], w_dn_ref[cols, :])
                                for r0, r1 in zip(inner[:-1], inner[1:])], axis=0)
        y = part if y is None else y + part
        c0 += cw
    o_ref[0] = x + gate * _rms(y, ng_ref[3:4])


def _layer_spec(stacked_shape, layer):
    nd = len(stacked_shape) - 1
    return pl.BlockSpec((None,) + tuple(stacked_shape[1:]), lambda *_: (layer,) + (0,) * nd,
                        pipeline_mode=pl.Buffered(1))


def _ffn(x, mod, ng, w_up, conv_w, conv_b, w_down, layer, tile=512):
    bsz, n_tok, d = x.shape
    ffn = w_down.shape[1]
    tile = min(tile, n_tok)
    n_tiles = n_tok // tile
    hb = tile // HALO
    n_hb = n_tok // HALO
    kern = functools.partial(_ffn_kernel, tile=tile, n_tiles=n_tiles, ffn=ffn)
    return pl.pallas_call(
        kern,
        grid=(bsz, n_tiles),
        in_specs=[pl.BlockSpec((1, tile, d), lambda b, i: (b, i, 0)),
                  pl.BlockSpec((1, HALO, d), lambda b, i: (b, jnp.maximum(i * hb - 1, 0), 0)),
                  pl.BlockSpec((1, HALO, d), lambda b, i: (b, jnp.minimum((i + 1) * hb, n_hb - 1), 0)),
                  pl.BlockSpec((1, 6, d), lambda b, i: (b, 0, 0)),
                  _const_spec(ng.shape),
                  _layer_spec(w_up.shape, layer),
                  _const_spec(conv_w.shape),
                  _const_spec((1, ffn)),
                  _layer_spec(w_down.shape, layer)],
        out_specs=pl.BlockSpec((1, tile, d), lambda b, i: (b, i, 0)),
        out_shape=jax.ShapeDtypeStruct(x.shape, F32),
        scratch_shapes=[pltpu.VMEM((tile + 2 * HALO, d), BF16)],
        compiler_params=_params(2),
        name="conv_glu",
    )(x, x, x, mod, ng, w_up, conv_w, conv_b.reshape(1, ffn), w_down)


def _qkv_kernel(x_ref, mod_ref, ng_ref, w_ref, b_ref, cos_ref, sin_ref, q_ref, kt_ref, v_ref,
                *, q_dim, kv_dim):
    x = x_ref[0]
    mod = mod_ref[0]
    sh, sc = mod[0:1], mod[1:2]
    h = (_rms(x, ng_ref[0:1]) * (1.0 + sc) + sh).astype(BF16)
    half = h.shape[0] // 2
    qkv = jnp.concatenate([_dot(h[0:half], w_ref[...]), _dot(h[half:], w_ref[...])], axis=0) + b_ref[...]
    cosf = cos_ref[...]
    sinf = sin_ref[...]
    lane = lax.broadcasted_iota(jnp.int32, cosf.shape, 1)
    low_half = (lane % HEAD_DIM) < (HEAD_DIM // 2)
    scale = HEAD_DIM ** -0.5 * LOG2E
    for j in range((q_dim + kv_dim) // 128):
        blk = qkv[:, j * 128:(j + 1) * 128]
        partner = jnp.where(low_half, pltpu.roll(blk, 96, 1), pltpu.roll(blk, 32, 1))
        r = blk * cosf + partner * sinf
        if j * 128 < q_dim:
            q_ref[0, :, j * 128:(j + 1) * 128] = (r * scale).astype(BF16)
        else:
            kt_ref[0, j * 128 - q_dim:(j + 1) * 128 - q_dim, :] = r.T.astype(BF16)
    v_ref[0] = qkv[:, q_dim + kv_dim:].astype(BF16)


def _qkv(x, mod, ng, w_qkv, b_qkv, cosf, sinf, tile=512):
    bsz, n_tok, d = x.shape
    tile = min(tile, n_tok)
    q_dim = N_HEADS * HEAD_DIM
    kv_dim = N_KV_HEADS * HEAD_DIM
    n = q_dim + 2 * kv_dim
    kern = functools.partial(_qkv_kernel, q_dim=q_dim, kv_dim=kv_dim)
    return pl.pallas_call(
        kern,
        grid=(bsz, n_tok // tile),
        in_specs=[pl.BlockSpec((1, tile, d), lambda b, i: (b, i, 0)),
                  pl.BlockSpec((1, 6, d), lambda b, i: (b, 0, 0)),
                  _const_spec(ng.shape),
                  _const_spec(w_qkv.shape),
                  _const_spec((1, n)),
                  pl.BlockSpec((tile, 128), lambda b, i: (i, 0)),
                  pl.BlockSpec((tile, 128), lambda b, i: (i, 0))],
        out_specs=[pl.BlockSpec((1, tile, q_dim), lambda b, i: (b, i, 0)),
                   pl.BlockSpec((1, kv_dim, tile), lambda b, i: (b, 0, i)),
                   pl.BlockSpec((1, tile, kv_dim), lambda b, i: (b, i, 0))],
        out_shape=[jax.ShapeDtypeStruct((bsz, n_tok, q_dim), BF16),
                   jax.ShapeDtypeStruct((bsz, kv_dim, n_tok), BF16),
                   jax.ShapeDtypeStruct((bsz, n_tok, kv_dim), BF16)],
        compiler_params=_params(2),
        name="qkv_rope",
    )(x, mod, ng, w_qkv, b_qkv.reshape(1, n), cosf, sinf)


def _rope_tables(n_tok):
    rows = n_tok // GRID_W
    n_freq = HEAD_DIM // 4
    row = jnp.repeat(jnp.arange(rows), GRID_W).astype(F32)
    col = jnp.tile(jnp.arange(GRID_W), rows).astype(F32)
    inv_freq = ROPE_THETA ** (-(jnp.arange(n_freq, dtype=F32) / n_freq))
    ang = jnp.concatenate([row[:, None] * inv_freq, col[:, None] * inv_freq], axis=-1)
    cos, sin = jnp.cos(ang), jnp.sin(ang)
    cosf = jnp.tile(cos, (1, 4))
    sinf = jnp.tile(jnp.concatenate([-sin, sin], axis=-1), (1, 2))
    return cosf, sinf


def _attn_kernel(sink_ref, q_ref, ktp_ref, ktc_ref, ktn_ref, vp_ref, vc_ref, vn_ref, kctx_ref, vctx_ref,
                 x_ref, mod_ref, ng_ref, wo_ref, o_ref, kwin_ref, vwin_ref, attn_ref, bias_ref, s_ref, p_ref,
                 m_ref, es_ref, ve_ref, vo_ref, vce_ref, vco_ref,
                 *, tq, n_tiles):
    i = pl.program_id(1)
    nb = tq // BLOCK
    kwin_ref[:, 0:BLOCK] = ktp_ref[0]
    kwin_ref[:, BLOCK:BLOCK + tq] = ktc_ref[0]
    kwin_ref[:, BLOCK + tq:] = ktn_ref[0]
    vwin_ref[0:BLOCK] = vp_ref[0]
    vwin_ref[BLOCK:BLOCK + tq] = vc_ref[0]
    vwin_ref[BLOCK + tq:] = vn_ref[0]
    band_w = 3 * BLOCK
    r = lax.broadcasted_iota(jnp.int32, (BLOCK, band_w), 0)
    c = lax.broadcasted_iota(jnp.int32, (BLOCK, band_w), 1)
    band = (c >= r) & (c <= r + 2 * BLOCK)
    first = band & ((c >= BLOCK) | (i > 0))
    last = band & ((c < 2 * BLOCK) | (i < n_tiles - 1))
    neg = jnp.float32(-jnp.inf)
    bias_ref[0, 0] = jnp.where(first, 0.0, neg)[:, 0:BLOCK]
    bias_ref[0, 1] = jnp.where(band, 0.0, neg)[:, 0:BLOCK]
    bias_ref[1, 0] = jnp.where(last, 0.0, neg)[:, 2 * BLOCK:]
    bias_ref[1, 1] = jnp.where(band, 0.0, neg)[:, 2 * BLOCK:]

    lane = lax.broadcasted_iota(jnp.int32, (1, 2 * HEAD_DIM), 1)
    low = lane < HEAD_DIM
    one = jnp.ones((), BF16)

    def pad_values(src_ref, even_ref, odd_ref, n_rows):
        for t in range(N_KV_HEADS // 2):
            src = src_ref[0:n_rows, t * 128:(t + 1) * 128]
            swapped = pltpu.roll(src, HEAD_DIM, 1)
            even_ref[0:n_rows, (2 * t) * 128:(2 * t + 1) * 128] = jnp.where(low, src, one)
            odd_ref[0:n_rows, (2 * t) * 128:(2 * t + 1) * 128] = jnp.where(low, one, swapped)
            even_ref[0:n_rows, (2 * t + 1) * 128:(2 * t + 2) * 128] = jnp.where(low, swapped, one)
            odd_ref[0:n_rows, (2 * t + 1) * 128:(2 * t + 2) * 128] = jnp.where(low, one, src)

    pad_values(vwin_ref, ve_ref, vo_ref, tq + 2 * BLOCK)
    pad_values(vctx_ref.at[0], vce_ref, vco_ref, vctx_ref.shape[1])

    n_rows = Q_PER_KV * BLOCK
    half = n_rows // 2
    row_order = (0, 2, 1, 3)
    unit = 0
    for b in range(nb):
        bp = 0 if b == 0 else 1
        bn = 0 if b == nb - 1 else 1
        q = q_ref[0, b * BLOCK:(b + 1) * BLOCK, :]
        for h in range(N_KV_HEADS):
            par = unit % 2
            unit += 1
            heads = [h * Q_PER_KV + g for g in row_order]
            feat = slice(h * HEAD_DIM, (h + 1) * HEAD_DIM)
            tile = slice(h * 128, (h + 1) * 128)
            keys = slice(b * BLOCK, b * BLOCK + band_w)
            qh = jnp.concatenate([q[:, n * HEAD_DIM:(n + 1) * HEAD_DIM] for n in heads], axis=0)
            s_ref[par, :, 0:band_w] = _dot(qh, kwin_ref[feat, keys])
            s_ref[par, :, band_w:] = _dot(qh, kctx_ref[0, feat, :])
            for ci in range(n_rows // SM_ROWS):
                rows = slice(ci * SM_ROWS, (ci + 1) * SM_ROWS)
                r0 = (ci * SM_ROWS) % BLOCK
                sink = sink_ref[heads[ci * SM_ROWS // BLOCK]] * LOG2E
                mt = jnp.maximum(s_ref[par, rows, 0:BLOCK] + bias_ref[0, bp, r0:r0 + SM_ROWS, :],
                                 s_ref[par, rows, 2 * BLOCK:band_w] + bias_ref[1, bn, r0:r0 + SM_ROWS, :])
                for t in (1, 3, 4):
                    mt = jnp.maximum(mt, s_ref[par, rows, t * BLOCK:(t + 1) * BLOCK])
                m = jnp.maximum(jnp.max(mt, axis=-1, keepdims=True), sink)
                m_ref[par, rows, :] = jnp.broadcast_to(m, (SM_ROWS, BLOCK))
                es_ref[par, rows, :] = jnp.broadcast_to(jnp.exp2(sink - m), (SM_ROWS, BLOCK))
            for ci in range(n_rows // SM_ROWS):
                rows = slice(ci * SM_ROWS, (ci + 1) * SM_ROWS)
                r0 = (ci * SM_ROWS) % BLOCK
                m = m_ref[par, rows, :]
                for t in range(5):
                    st = s_ref[par, rows, t * BLOCK:(t + 1) * BLOCK]
                    if t == 0:
                        st = st + bias_ref[0, bp, r0:r0 + SM_ROWS, :]
                    if t == 2:
                        st = st + bias_ref[1, bn, r0:r0 + SM_ROWS, :]
                    p_ref[par, rows, t * BLOCK:(t + 1) * BLOCK] = jnp.exp2(st - m).astype(BF16)
            for part, (v_ref, vc_ref) in enumerate(((ve_ref, vce_ref), (vo_ref, vco_ref))):
                rows = slice(part * half, (part + 1) * half)
                acc = (_dot(p_ref[par, rows, 0:band_w], v_ref[keys, tile])
                       + _dot(p_ref[par, rows, band_w:], vc_ref[:, tile]))
                denom = pltpu.roll(acc, HEAD_DIM, 1) + es_ref[par, rows, :]
                o_ref_half = acc / denom
                if part == 0:
                    o_even = o_ref_half
                else:
                    o_odd = o_ref_half
            for gp in range(Q_PER_KV // 2):
                pair = jnp.where(low, o_even[gp * BLOCK:(gp + 1) * BLOCK], o_odd[gp * BLOCK:(gp + 1) * BLOCK])
                col = (h * Q_PER_KV + 2 * gp) * HEAD_DIM
                attn_ref[b * BLOCK:(b + 1) * BLOCK, col:col + 2 * HEAD_DIM] = pair.astype(BF16)
    y = _dot(attn_ref[...], wo_ref[...])
    gate = mod_ref[0][2:3]
    o_ref[0] = x_ref[0] + gate * _rms(y, ng_ref[1:2])


def _attn(x, q, kt, v, kctx_t, vctx, sink, mod, ng, w_o, tq=512):
    bsz, n_tok, d = x.shape
    n_tiles = n_tok // tq
    nb = tq // BLOCK
    n_blk = n_tok // BLOCK
    n_ctx = vctx.shape[1]
    q_dim = q.shape[2]
    kv_dim = v.shape[2]
    kern = functools.partial(_attn_kernel, tq=tq, n_tiles=n_tiles)
    cur_map = lambda b, i: (b, i, 0)
    batch_map = lambda b, i: (b, 0, 0)
    return pl.pallas_call(
        kern,
        grid=(bsz, n_tiles),
        in_specs=[pl.BlockSpec(memory_space=pltpu.SMEM),
                  pl.BlockSpec((1, tq, q_dim), cur_map),
                  pl.BlockSpec((1, kv_dim, BLOCK), lambda b, i: (b, 0, jnp.maximum(i * nb - 1, 0))),
                  pl.BlockSpec((1, kv_dim, tq), lambda b, i: (b, 0, i)),
                  pl.BlockSpec((1, kv_dim, BLOCK), lambda b, i: (b, 0, jnp.minimum((i + 1) * nb, n_blk - 1))),
                  pl.BlockSpec((1, BLOCK, kv_dim), lambda b, i: (b, jnp.maximum(i * nb - 1, 0), 0)),
                  pl.BlockSpec((1, tq, kv_dim), cur_map),
                  pl.BlockSpec((1, BLOCK, kv_dim), lambda b, i: (b, jnp.minimum((i + 1) * nb, n_blk - 1), 0)),
                  pl.BlockSpec((1, kv_dim, n_ctx), batch_map),
                  pl.BlockSpec((1, n_ctx, kv_dim), batch_map),
                  pl.BlockSpec((1, tq, d), cur_map),
                  pl.BlockSpec((1, 6, d), batch_map),
                  _const_spec(ng.shape),
                  _const_spec(w_o.shape)],
        out_specs=pl.BlockSpec((1, tq, d), cur_map),
        out_shape=jax.ShapeDtypeStruct(x.shape, F32),
        scratch_shapes=[pltpu.VMEM((kv_dim, tq + 2 * BLOCK), BF16),
                        pltpu.VMEM((tq + 2 * BLOCK, kv_dim), BF16),
                        pltpu.VMEM((tq, q_dim), BF16),
                        pltpu.VMEM((2, 2, BLOCK, BLOCK), F32),
                        pltpu.VMEM((2, Q_PER_KV * BLOCK, 3 * BLOCK + n_ctx), F32),
                        pltpu.VMEM((2, Q_PER_KV * BLOCK, 3 * BLOCK + n_ctx), BF16),
                        pltpu.VMEM((2, Q_PER_KV * BLOCK, BLOCK), F32),
                        pltpu.VMEM((2, Q_PER_KV * BLOCK, BLOCK), F32),
                        pltpu.VMEM((tq + 2 * BLOCK, N_KV_HEADS * 128), BF16),
                        pltpu.VMEM((tq + 2 * BLOCK, N_KV_HEADS * 128), BF16),
                        pltpu.VMEM((n_ctx, N_KV_HEADS * 128), BF16),
                        pltpu.VMEM((n_ctx, N_KV_HEADS * 128), BF16)],
        compiler_params=_params(2),
        name="window_attn",
    )(sink, q, kt, kt, kt, v, v, v, kctx_t, vctx, x, mod, ng, w_o)


def kernel(x, c, ctx, c_ctx, ada_w, ada_b, norm_g, a_w_in, a_ln_g, a_ln_b, a_w_s, a_b_s, a_w_out,
           b_w_qkv, b_b_qkv, b_sink, b_w_o, f_w_up, f_conv_w, f_conv_b, f_w_down):
    bsz, n_tok, d = x.shape
    n_ctx = ctx.shape[1]
    depth = ada_w.shape[0]
    assert depth == 2, "layer 0 = gMLP mixer, layer 1 = windowed attention (last layer: no context update)"

    rows = 8 * ((bsz + 1 + 7) // 8)
    cond = jnp.zeros((rows, d), F32).at[:bsz].set(c).at[bsz].set(c_ctx)
    mods = _ada(cond, ada_w, ada_b)

    def lat_mod(i):
        return mods[i, :bsz].reshape(bsz, 6, d)

    def ctx_mod(i):
        return jnp.broadcast_to(mods[i, bsz].reshape(1, 6, d), (bsz, 6, d))

    def ctx_mod_flat(i):
        return mods[i, bsz].reshape(1, 6, d)

    bf = lambda w: w.astype(BF16)
    w_up, w_down = bf(f_w_up), bf(f_w_down)

    ctx_flat = lambda a: a.reshape(1, bsz * n_ctx, a.shape[-1])

    gm = (norm_g[0], bf(a_w_in[0]), a_ln_g[0], a_ln_b[0], bf(a_w_s[0]), a_b_s[0], bf(a_w_out[0]))
    ff0 = (norm_g[0], w_up, f_conv_w[0], f_conv_b[0], w_down, 0)
    x = _gmlp(x, lat_mod(0), *gm)
    ctx = _gmlp(ctx_flat(ctx), ctx_mod_flat(0), *gm).reshape(bsz, n_ctx, d)
    x = _ffn(x, lat_mod(0), *ff0)
    ctx = _ffn(ctx, ctx_mod(0), *ff0)

    cosf, sinf = _rope_tables(n_tok)
    w_qkv = bf(b_w_qkv[0])
    q, k, v = _qkv(x, lat_mod(1), norm_g[1], w_qkv, b_b_qkv[0], cosf, sinf)
    _, kctx, vctx = _qkv(ctx_flat(ctx), ctx_mod_flat(1), norm_g[1], w_qkv, b_b_qkv[0],
                         jnp.ones((bsz * n_ctx, 128), F32), jnp.zeros((bsz * n_ctx, 128), F32))
    kv_dim = kctx.shape[1]
    kctx = kctx.reshape(kv_dim, bsz, n_ctx).transpose(1, 0, 2)
    vctx = vctx.reshape(bsz, n_ctx, kv_dim)
    x = _attn(x, q, k, v, kctx, vctx, b_sink[0], lat_mod(1), norm_g[1], bf(b_w_o[0]))
    x = _ffn(x, lat_mod(1), norm_g[1], w_up, f_conv_w[1], f_conv_b[1], w_down, 1)
    return x
```

```python
import functools
import math

import jax
import jax.numpy as jnp
from jax import lax
from jax.experimental import pallas as pl
from jax.experimental.pallas import tpu as pltpu

EPS = 1e-6
GRID_W = 64
CHUNK = 128
A_GROUPS = 8
HEAD_DIM = 64
N_HEADS = 16
N_KV_HEADS = 4
Q_PER_KV = N_HEADS // N_KV_HEADS
BLOCK = 128
ROPE_THETA = 10000.0
LOG2E = math.log2(math.e)
SM_ROWS = 32
GMLP_CHUNK = 512
GMLP_OUT_GROUPS = 8
GMLP_SPLIT = 2
FFN_CHUNK = 2816
FFN_SPLIT = 2
HALO = 16
VMEM_LIMIT = 56 * 1024 * 1024

F32 = jnp.float32
BF16 = jnp.bfloat16


def _rms(x, g):
    return x * lax.rsqrt(jnp.mean(x * x, axis=-1, keepdims=True) + EPS) * g


def _gelu(x):
    c = math.sqrt(2.0 / math.pi)
    return x * (0.5 * (1.0 + jnp.tanh(c * (x + 0.044715 * (x * x * x)))))


def _dot(a, b):
    return jnp.dot(a, b, preferred_element_type=F32)


def _params(n_grid):
    return pltpu.CompilerParams(dimension_semantics=("parallel",) * n_grid,
                                vmem_limit_bytes=VMEM_LIMIT)


def _const_spec(shape):
    nd = len(shape)
    return pl.BlockSpec(shape, lambda *_: (0,) * nd, pipeline_mode=pl.Buffered(1))


def _ada_kernel(s_ref, w_ref, b_ref, o_ref):
    s = s_ref[...]
    s = s * jax.nn.sigmoid(s)
    o_ref[0] = _dot(s.astype(BF16), w_ref[0].astype(BF16)) + b_ref[0]


def _ada(cond, ada_w, ada_b, tn=1536):
    depth, d, n = ada_w.shape
    r = cond.shape[0]
    return pl.pallas_call(
        _ada_kernel,
        grid=(depth, n // tn),
        in_specs=[pl.BlockSpec((r, d), lambda i, j: (0, 0)),
                  pl.BlockSpec((1, d, tn), lambda i, j: (i, 0, j)),
                  pl.BlockSpec((1, 1, tn), lambda i, j: (i, 0, j))],
        out_specs=pl.BlockSpec((1, r, tn), lambda i, j: (i, 0, j)),
        out_shape=jax.ShapeDtypeStruct((depth, r, n), F32),
        compiler_params=_params(2),
        name="ada_mod",
    )(cond, ada_w, ada_b.reshape(depth, 1, n))


def _gmlp_kernel(x_ref, mod_ref, ng_ref, w_in_ref, lng_ref, lnb_ref, ws_ref, bs_ref, w_out_ref,
                 o_ref, h_ref, v_ref, t_ref, *, tile, width):
    x = x_ref[0]
    mod = mod_ref[0]
    sh, sc, gate = mod[0:1], mod[1:2], mod[2:3]
    h_ref[...] = (_rms(x, ng_ref[0:1]) * (1.0 + sc) + sh).astype(BF16)
    half = max(tile // GMLP_SPLIT, CHUNK)

    def dot_rows(lhs_ref, w):
        return jnp.concatenate([_dot(lhs_ref[r0:r0 + half], w) for r0 in range(0, tile, half)], axis=0)

    vsum = jnp.zeros((tile, 1), F32)
    for c0 in range(0, width, GMLP_CHUNK):
        vj = _gelu(dot_rows(h_ref, w_in_ref[:, width + c0:width + c0 + GMLP_CHUNK]))
        v_ref[:, c0:c0 + GMLP_CHUNK] = vj
        vsum = vsum + jnp.sum(vj, axis=-1, keepdims=True)
    mu = vsum * (1.0 / width)
    vss = jnp.zeros((tile, 1), F32)
    for c0 in range(0, width, GMLP_CHUNK):
        dv = v_ref[:, c0:c0 + GMLP_CHUNK] - mu
        vss = vss + jnp.sum(dv * dv, axis=-1, keepdims=True)
    rstd = lax.rsqrt(vss * (1.0 / width) + EPS)
    gw = width // A_GROUPS
    y = None
    for g in range(A_GROUPS):
        cols = slice(g * gw, (g + 1) * gw)
        vn = ((v_ref[:, cols] - mu) * rstd * lng_ref[:, cols] + lnb_ref[:, cols]).astype(BF16)
        u = _gelu(dot_rows(h_ref, w_in_ref[:, cols]))
        s = jnp.concatenate([_dot(ws_ref[g], vn[r0:r0 + CHUNK]) + bs_ref[g]
                             for r0 in range(0, tile, CHUNK)], axis=0)
        t_ref[:, cols] = (u * s).astype(BF16)
        if (g + 1) % GMLP_OUT_GROUPS == 0:
            k0 = (g + 1 - GMLP_OUT_GROUPS) * gw
            part = dot_rows(t_ref.at[:, k0:(g + 1) * gw], w_out_ref[k0:(g + 1) * gw, :])
            y = part if y is None else y + part
    o_ref[0] = x + gate * _rms(y, ng_ref[1:2])


def _gmlp(x, mod, ng, w_in, ln_g, ln_b, w_s, b_s, w_out, tile=512):
    bsz, n_tok, d = x.shape
    width = w_out.shape[0]
    gw = width // A_GROUPS
    tile = min(tile, n_tok)
    bs_full = jnp.broadcast_to(b_s[:, :, None], (A_GROUPS, CHUNK, gw))
    kern = functools.partial(_gmlp_kernel, tile=tile, width=width)
    return pl.pallas_call(
        kern,
        grid=(bsz, n_tok // tile),
        in_specs=[pl.BlockSpec((1, tile, d), lambda b, i: (b, i, 0)),
                  pl.BlockSpec((1, 6, d), lambda b, i: (b, 0, 0)),
                  _const_spec(ng.shape),
                  _const_spec(w_in.shape),
                  _const_spec((1, width)),
                  _const_spec((1, width)),
                  _const_spec(w_s.shape),
                  _const_spec(bs_full.shape),
                  _const_spec(w_out.shape)],
        out_specs=pl.BlockSpec((1, tile, d), lambda b, i: (b, i, 0)),
        out_shape=jax.ShapeDtypeStruct(x.shape, F32),
        scratch_shapes=[pltpu.VMEM((tile, d), BF16),
                        pltpu.VMEM((tile, width), F32),
                        pltpu.VMEM((tile, width), BF16)],
        compiler_params=_params(2),
        name="gmlp_mixer",
    )(x, mod, ng, w_in, ln_g.reshape(1, width), ln_b.reshape(1, width), w_s, bs_full, w_out)


def _ffn_kernel(x_ref, xp_ref, xn_ref, mod_ref, ng_ref, w_up_ref, cw_ref, cb_ref, w_dn_ref,
                o_ref, h_ref, *, tile, n_tiles, ffn):
    i = pl.program_id(1)
    mod = mod_ref[0]
    sh, sc, gate = mod[3:4], mod[4:5], mod[5:6]
    g_pre = ng_ref[2:3]

    def pre(xx):
        return _rms(xx, g_pre) * (1.0 + sc) + sh

    x = x_ref[0]
    has_prev = (i > 0).astype(F32)
    has_next = (i < n_tiles - 1).astype(F32)
    h_ref[0:HALO] = (pre(xp_ref[0]) * has_prev).astype(BF16)
    h_ref[HALO:HALO + tile] = pre(x).astype(BF16)
    h_ref[HALO + tile:] = (pre(xn_ref[0]) * has_next).astype(BF16)
    ext = tile + 2 * HALO
    y = None
    c0 = 0
    while c0 < ffn:
        cw = min(FFN_CHUNK, ffn - c0)
        cols = slice(c0, c0 + cw)
        cuts = [0] + [HALO + (tile * k) // FFN_SPLIT for k in range(1, FFN_SPLIT)] + [ext]
        inner = [HALO] + cuts[1:-1] + [HALO + tile]
        a_ext = jnp.concatenate([_dot(h_ref[r0:r1], w_up_ref[:, cols])
                                 for r0, r1 in zip(cuts[:-1], cuts[1:])], axis=0)
        a = (pltpu.roll(a_ext, 1, 0)[HALO:HALO + tile] * cw_ref[0:1, cols]
             + a_ext[HALO:HALO + tile] * cw_ref[1:2, cols]
             + pltpu.roll(a_ext, ext - 1, 0)[HALO:HALO + tile] * cw_ref[2:3, cols]
             + cb_ref[:, cols])
        w_b = w_up_ref[:, ffn + c0:ffn + c0 + cw]
        b = jnp.concatenate([_dot(h_ref[r0:r1], w_b) for r0, r1 in zip(inner[:-1], inner[1:])], axis=0)
        g = (_gelu(a) * b).astype(BF16)
        part = jnp.concatenate([_dot(g[r0 - HALO:r1 - HALO], w_dn_ref[cols, :])
                                for r0, r1 in zip(inner[:-1], inner[1:])], axis=0)
        y = part if y is None else y + part
        c0 += cw
    o_ref[0] = x + gate * _rms(y, ng_ref[3:4])


def _layer_spec(stacked_shape, layer):
    nd = len(stacked_shape) - 1
    return pl.BlockSpec((None,) + tuple(stacked_shape[1:]), lambda *_: (layer,) + (0,) * nd,
                        pipeline_mode=pl.Buffered(1))


def _ffn(x, mod, ng, w_up, conv_w, conv_b, w_down, layer, tile=512):
    bsz, n_tok, d = x.shape
    ffn = w_down.shape[1]
    tile = min(tile, n_tok)
    n_tiles = n_tok // tile
    hb = tile // HALO
    n_hb = n_tok // HALO
    kern = functools.partial(_ffn_kernel, tile=tile, n_tiles=n_tiles, ffn=ffn)
    return pl.pallas_call(
        kern,
        grid=(bsz, n_tiles),
        in_specs=[pl.BlockSpec((1, tile, d), lambda b, i: (b, i, 0)),
                  pl.BlockSpec((1, HALO, d), lambda b, i: (b, jnp.maximum(i * hb - 1, 0), 0)),
                  pl.BlockSpec((1, HALO, d), lambda b, i: (b, jnp.minimum((i + 1) * hb, n_hb - 1), 0)),
                  pl.BlockSpec((1, 6, d), lambda b, i: (b, 0, 0)),
                  _const_spec(ng.shape),
                  _layer_spec(w_up.shape, layer),
                  _const_spec(conv_w.shape),
                  _const_spec((1, ffn)),
                  _layer_spec(w_down.shape, layer)],
        out_specs=pl.BlockSpec((1, tile, d), lambda b, i: (b, i, 0)),
        out_shape=jax.ShapeDtypeStruct(x.shape, F32),
        scratch_shapes=[pltpu.VMEM((tile + 2 * HALO, d), BF16)],
        compiler_params=_params(2),
        name="conv_glu",
    )(x, x, x, mod, ng, w_up, conv_w, conv_b.reshape(1, ffn), w_down)


def _qkv_kernel(x_ref, mod_ref, ng_ref, w_ref, b_ref, cos_ref, sin_ref, q_ref, kt_ref, v_ref,
                *, q_dim, kv_dim):
    x = x_ref[0]
    mod = mod_ref[0]
    sh, sc = mod[0:1], mod[1:2]
    h = (_rms(x, ng_ref[0:1]) * (1.0 + sc) + sh).astype(BF16)
    half = h.shape[0] // 2
    qkv = jnp.concatenate([_dot(h[0:half], w_ref[...]), _dot(h[half:], w_ref[...])], axis=0) + b_ref[...]
    cosf = cos_ref[...]
    sinf = sin_ref[...]
    lane = lax.broadcasted_iota(jnp.int32, cosf.shape, 1)
    low_half = (lane % HEAD_DIM) < (HEAD_DIM // 2)
    scale = HEAD_DIM ** -0.5 * LOG2E
    for j in range((q_dim + kv_dim) // 128):
        blk = qkv[:, j * 128:(j + 1) * 128]
        partner = jnp.where(low_half, pltpu.roll(blk, 96, 1), pltpu.roll(blk, 32, 1))
        r = blk * cosf + partner * sinf
        if j * 128 < q_dim:
            q_ref[0, :, j * 128:(j + 1) * 128] = (r * scale).astype(BF16)
        else:
            kt_ref[0, j * 128 - q_dim:(j + 1) * 128 - q_dim, :] = r.T.astype(BF16)
    v_ref[0] = qkv[:, q_dim + kv_dim:].astype(BF16)


def _qkv(x, mod, ng, w_qkv, b_qkv, cosf, sinf, tile=512):
    bsz, n_tok, d = x.shape
    tile = min(tile, n_tok)
    q_dim = N_HEADS * HEAD_DIM
    kv_dim = N_KV_HEADS * HEAD_DIM
    n = q_dim + 2 * kv_dim
    kern = functools.partial(_qkv_kernel, q_dim=q_dim, kv_dim=kv_dim)
    return pl.pallas_call(
        kern,
        grid=(bsz, n_tok // tile),
        in_specs=[pl.BlockSpec((1, tile, d), lambda b, i: (b, i, 0)),
                  pl.BlockSpec((1, 6, d), lambda b, i: (b, 0, 0)),
                  _const_spec(ng.shape),
                  _const_spec(w_qkv.shape),
                  _const_spec((1, n)),
                  pl.BlockSpec((tile, 128), lambda b, i: (i, 0)),
                  pl.BlockSpec((tile, 128), lambda b, i: (i, 0))],
        out_specs=[pl.BlockSpec((1, tile, q_dim), lambda b, i: (b, i, 0)),
                   pl.BlockSpec((1, kv_dim, tile), lambda b, i: (b, 0, i)),
                   pl.BlockSpec((1, tile, kv_dim), lambda b, i: (b, i, 0))],
        out_shape=[jax.ShapeDtypeStruct((bsz, n_tok, q_dim), BF16),
                   jax.ShapeDtypeStruct((bsz, kv_dim, n_tok), BF16),
                   jax.ShapeDtypeStruct((bsz, n_tok, kv_dim), BF16)],
        compiler_params=_params(2),
        name="qkv_rope",
    )(x, mod, ng, w_qkv, b_qkv.reshape(1, n), cosf, sinf)


def _rope_tables(n_tok):
    rows = n_tok // GRID_W
    n_freq = HEAD_DIM // 4
    row = jnp.repeat(jnp.arange(rows), GRID_W).astype(F32)
    col = jnp.tile(jnp.arange(GRID_W), rows).astype(F32)
    inv_freq = ROPE_THETA ** (-(jnp.arange(n_freq, dtype=F32) / n_freq))
    ang = jnp.concatenate([row[:, None] * inv_freq, col[:, None] * inv_freq], axis=-1)
    cos, sin = jnp.cos(ang), jnp.sin(ang)
    cosf = jnp.tile(cos, (1, 4))
    sinf = jnp.tile(jnp.concatenate([-sin, sin], axis=-1), (1, 2))
    return cosf, sinf


def _attn_kernel(sink_ref, q_ref, ktp_ref, ktc_ref, ktn_ref, vp_ref, vc_ref, vn_ref, kctx_ref, vctx_ref,
                 x_ref, mod_ref, ng_ref, wo_ref, o_ref, kwin_ref, vwin_ref, attn_ref, bias_ref, s_ref, p_ref,
                 m_ref, es_ref, ve_ref, vo_ref, vce_ref, vco_ref,
                 *, tq, n_tiles):
    i = pl.program_id(1)
    nb = tq // BLOCK
    kwin_ref[:, 0:BLOCK] = ktp_ref[0]
    kwin_ref[:, BLOCK:BLOCK + tq] = ktc_ref[0]
    kwin_ref[:, BLOCK + tq:] = ktn_ref[0]
    vwin_ref[0:BLOCK] = vp_ref[0]
    vwin_ref[BLOCK:BLOCK + tq] = vc_ref[0]
    vwin_ref[BLOCK + tq:] = vn_ref[0]
    band_w = 3 * BLOCK
    r = lax.broadcasted_iota(jnp.int32, (BLOCK, band_w), 0)
    c = lax.broadcasted_iota(jnp.int32, (BLOCK, band_w), 1)
    band = (c >= r) & (c <= r + 2 * BLOCK)
    first = band & ((c >= BLOCK) | (i > 0))
    last = band & ((c < 2 * BLOCK) | (i < n_tiles - 1))
    neg = jnp.float32(-jnp.inf)
    bias_ref[0, 0] = jnp.where(first, 0.0, neg)[:, 0:BLOCK]
    bias_ref[0, 1] = jnp.where(band, 0.0, neg)[:, 0:BLOCK]
    bias_ref[1, 0] = jnp.where(last, 0.0, neg)[:, 2 * BLOCK:]
    bias_ref[1, 1] = jnp.where(band, 0.0, neg)[:, 2 * BLOCK:]

    lane = lax.broadcasted_iota(jnp.int32, (1, 2 * HEAD_DIM), 1)
    low = lane < HEAD_DIM
    one = jnp.ones((), BF16)

    def pad_values(src_ref, even_ref, odd_ref, n_rows):
        for t in range(N_KV_HEADS // 2):
            src = src_ref[0:n_rows, t * 128:(t + 1) * 128]
            swapped = pltpu.roll(src, HEAD_DIM, 1)
            even_ref[0:n_rows, (2 * t) * 128:(2 * t + 1) * 128] = jnp.where(low, src, one)
            odd_ref[0:n_rows, (2 * t) * 128:(2 * t + 1) * 128] = jnp.where(low, one, swapped)
            even_ref[0:n_rows, (2 * t + 1) * 128:(2 * t + 2) * 128] = jnp.where(low, swapped, one)
            odd_ref[0:n_rows, (2 * t + 1) * 128:(2 * t + 2) * 128] = jnp.where(low, one, src)

    pad_values(vwin_ref, ve_ref, vo_ref, tq + 2 * BLOCK)
    pad_values(vctx_ref.at[0], vce_ref, vco_ref, vctx_ref.shape[1])

    n_rows = Q_PER_KV * BLOCK
    half = n_rows // 2
    row_order = (0, 2, 1, 3)
    unit = 0
    for b in range(nb):
        bp = 0 if b == 0 else 1
        bn = 0 if b == nb - 1 else 1
        q = q_ref[0, b * BLOCK:(b + 1) * BLOCK, :]
        for h in range(N_KV_HEADS):
            par = unit % 2
            unit += 1
            heads = [h * Q_PER_KV + g for g in row_order]
            feat = slice(h * HEAD_DIM, (h + 1) * HEAD_DIM)
            tile = slice(h * 128, (h + 1) * 128)
            keys = slice(b * BLOCK, b * BLOCK + band_w)
            qh = jnp.concatenate([q[:, n * HEAD_DIM:(n + 1) * HEAD_DIM] for n in heads], axis=0)
            s_ref[par, :, 0:band_w] = _dot(qh, kwin_ref[feat, keys])
            s_ref[par, :, band_w:] = _dot(qh, kctx_ref[0, feat, :])
            for ci in range(n_rows // SM_ROWS):
                rows = slice(ci * SM_ROWS, (ci + 1) * SM_ROWS)
                r0 = (ci * SM_ROWS) % BLOCK
                sink = sink_ref[heads[ci * SM_ROWS // BLOCK]] * LOG2E
                mt = jnp.maximum(s_ref[par, rows, 0:BLOCK] + bias_ref[0, bp, r0:r0 + SM_ROWS, :],
                                 s_ref[par, rows, 2 * BLOCK:band_w] + bias_ref[1, bn, r0:r0 + SM_ROWS, :])
                for t in (1, 3, 4):
                    mt = jnp.maximum(mt, s_ref[par, rows, t * BLOCK:(t + 1) * BLOCK])
                m = jnp.maximum(jnp.max(mt, axis=-1, keepdims=True), sink)
                m_ref[par, rows, :] = jnp.broadcast_to(m, (SM_ROWS, BLOCK))
                es_ref[par, rows, :] = jnp.broadcast_to(jnp.exp2(sink - m), (SM_ROWS, BLOCK))
            for ci in range(n_rows // SM_ROWS):
                rows = slice(ci * SM_ROWS, (ci + 1) * SM_ROWS)
                r0 = (ci * SM_ROWS) % BLOCK
                m = m_ref[par, rows, :]
                for t in range(5):
                    st = s_ref[par, rows, t * BLOCK:(t + 1) * BLOCK]
                    if t == 0:
                        st = st + bias_ref[0, bp, r0:r0 + SM_ROWS, :]
                    if t == 2:
                        st = st + bias_ref[1, bn, r0:r0 + SM_ROWS, :]
                    p_ref[par, rows, t * BLOCK:(t + 1) * BLOCK] = jnp.exp2(st - m).astype(BF16)
            for part, (v_ref, vc_ref) in enumerate(((ve_ref, vce_ref), (vo_ref, vco_ref))):
                rows = slice(part * half, (part + 1) * half)
                acc = (_dot(p_ref[par, rows, 0:band_w], v_ref[keys, tile])
                       + _dot(p_ref[par, rows, band_w:], vc_ref[:, tile]))
                denom = pltpu.roll(acc, HEAD_DIM, 1) + es_ref[par, rows, :]
                o_ref_half = acc / denom
                if part == 0:
                    o_even = o_ref_half
                else:
                    o_odd = o_ref_half
            for gp in range(Q_PER_KV // 2):
                pair = jnp.where(low, o_even[gp * BLOCK:(gp + 1) * BLOCK], o_odd[gp * BLOCK:(gp + 1) * BLOCK])
                col = (h * Q_PER_KV + 2 * gp) * HEAD_DIM
                attn_ref[b * BLOCK:(b + 1) * BLOCK, col:col + 2 * HEAD_DIM] = pair.astype(BF16)
    y = _dot(attn_ref[...], wo_ref[...])
    gate = mod_ref[0][2:3]
    o_ref[0] = x_ref[0] + gate * _rms(y, ng_ref[1:2])


def _attn(x, q, kt, v, kctx_t, vctx, sink, mod, ng, w_o, tq=1024):
    bsz, n_tok, d = x.shape
    n_tiles = n_tok // tq
    nb = tq // BLOCK
    n_blk = n_tok // BLOCK
    n_ctx = vctx.shape[1]
    q_dim = q.shape[2]
    kv_dim = v.shape[2]
    kern = functools.partial(_attn_kernel, tq=tq, n_tiles=n_tiles)
    cur_map = lambda b, i: (b, i, 0)
    batch_map = lambda b, i: (b, 0, 0)
    return pl.pallas_call(
        kern,
        grid=(bsz, n_tiles),
        in_specs=[pl.BlockSpec(memory_space=pltpu.SMEM),
                  pl.BlockSpec((1, tq, q_dim), cur_map),
                  pl.BlockSpec((1, kv_dim, BLOCK), lambda b, i: (b, 0, jnp.maximum(i * nb - 1, 0))),
                  pl.BlockSpec((1, kv_dim, tq), lambda b, i: (b, 0, i)),
                  pl.BlockSpec((1, kv_dim, BLOCK), lambda b, i: (b, 0, jnp.minimum((i + 1) * nb, n_blk - 1))),
                  pl.BlockSpec((1, BLOCK, kv_dim), lambda b, i: (b, jnp.maximum(i * nb - 1, 0), 0)),
                  pl.BlockSpec((1, tq, kv_dim), cur_map),
                  pl.BlockSpec((1, BLOCK, kv_dim), lambda b, i: (b, jnp.minimum((i + 1) * nb, n_blk - 1), 0)),
                  pl.BlockSpec((1, kv_dim, n_ctx), batch_map),
                  pl.BlockSpec((1, n_ctx, kv_dim), batch_map),
                  pl.BlockSpec((1, tq, d), cur_map),
                  pl.BlockSpec((1, 6, d), batch_map),
                  _const_spec(ng.shape),
                  _const_spec(w_o.shape)],
        out_specs=pl.BlockSpec((1, tq, d), cur_map),
        out_shape=jax.ShapeDtypeStruct(x.shape, F32),
        scratch_shapes=[pltpu.VMEM((kv_dim, tq + 2 * BLOCK), BF16),
                        pltpu.VMEM((tq + 2 * BLOCK, kv_dim), BF16),
                        pltpu.VMEM((tq, q_dim), BF16),
                        pltpu.VMEM((2, 2, BLOCK, BLOCK), F32),
                        pltpu.VMEM((2, Q_PER_KV * BLOCK, 3 * BLOCK + n_ctx), F32),
                        pltpu.VMEM((2, Q_PER_KV * BLOCK, 3 * BLOCK + n_ctx), BF16),
                        pltpu.VMEM((2, Q_PER_KV * BLOCK, BLOCK), F32),
                        pltpu.VMEM((2, Q_PER_KV * BLOCK, BLOCK), F32),
                        pltpu.VMEM((tq + 2 * BLOCK, N_KV_HEADS * 128), BF16),
                        pltpu.VMEM((tq + 2 * BLOCK, N_KV_HEADS * 128), BF16),
                        pltpu.VMEM((n_ctx, N_KV_HEADS * 128), BF16),
                        pltpu.VMEM((n_ctx, N_KV_HEADS * 128), BF16)],
        compiler_params=_params(2),
        name="window_attn",
    )(sink, q, kt, kt, kt, v, v, v, kctx_t, vctx, x, mod, ng, w_o)


def kernel(x, c, ctx, c_ctx, ada_w, ada_b, norm_g, a_w_in, a_ln_g, a_ln_b, a_w_s, a_b_s, a_w_out,
           b_w_qkv, b_b_qkv, b_sink, b_w_o, f_w_up, f_conv_w, f_conv_b, f_w_down):
    bsz, n_tok, d = x.shape
    n_ctx = ctx.shape[1]
    depth = ada_w.shape[0]
    assert depth == 2, "layer 0 = gMLP mixer, layer 1 = windowed attention (last layer: no context update)"

    rows = 8 * ((bsz + 1 + 7) // 8)
    cond = jnp.zeros((rows, d), F32).at[:bsz].set(c).at[bsz].set(c_ctx)
    mods = _ada(cond, ada_w, ada_b)

    def lat_mod(i):
        return mods[i, :bsz].reshape(bsz, 6, d)

    def ctx_mod(i):
        return jnp.broadcast_to(mods[i, bsz].reshape(1, 6, d), (bsz, 6, d))

    def ctx_mod_flat(i):
        return mods[i, bsz].reshape(1, 6, d)

    bf = lambda w: w.astype(BF16)
    w_up, w_down = bf(f_w_up), bf(f_w_down)

    ctx_flat = lambda a: a.reshape(1, bsz * n_ctx, a.shape[-1])

    gm = (norm_g[0], bf(a_w_in[0]), a_ln_g[0], a_ln_b[0], bf(a_w_s[0]), a_b_s[0], bf(a_w_out[0]))
    ff0 = (norm_g[0], w_up, f_conv_w[0], f_conv_b[0], w_down, 0)
    x = _gmlp(x, lat_mod(0), *gm)
    ctx = _gmlp(ctx_flat(ctx), ctx_mod_flat(0), *gm).reshape(bsz, n_ctx, d)
    x = _ffn(x, lat_mod(0), *ff0)
    ctx = _ffn(ctx, ctx_mod(0), *ff0)

    cosf, sinf = _rope_tables(n_tok)
    w_qkv = bf(b_w_qkv[0])
    q, k, v = _qkv(x, lat_mod(1), norm_g[1], w_qkv, b_b_qkv[0], cosf, sinf)
    _, kctx, vctx = _qkv(ctx_flat(ctx), ctx_mod_flat(1), norm_g[1], w_qkv, b_b_qkv[0],
                         jnp.ones((bsz * n_ctx, 128), F32), jnp.zeros((bsz * n_ctx, 128), F32))
    kv_dim = kctx.shape[1]
    kctx = kctx.reshape(kv_dim, bsz, n_ctx).transpose(1, 0, 2)
    vctx = vctx.reshape(bsz, n_ctx, kv_dim)
    x = _attn(x, q, k, v, kctx, vctx, b_sink[0], lat_mod(1), norm_g[1], bf(b_w_o[0]))
    x = _ffn(x, lat_mod(1), norm_g[1], w_up, f_conv_w[1], f_conv_b[1], w_down, 1)
    return x
```

```python
import functools
import math

import jax
import jax.numpy as jnp
from jax import lax
from jax.experimental import pallas as pl
from jax.experimental.pallas import tpu as pltpu

EPS = 1e-6
GRID_W = 64
CHUNK = 128
A_GROUPS = 8
HEAD_DIM = 64
N_HEADS = 16
N_KV_HEADS = 4
Q_PER_KV = N_HEADS // N_KV_HEADS
BLOCK = 128
ROPE_THETA = 10000.0
LANES = 128
LOG2E = math.log2(math.e)
SM_ROWS = 32
GMLP_CHUNK = 512
GMLP_OUT_GROUPS = 8
GMLP_SPLIT = 2
FFN_CHUNK = 2816
FFN_SPLIT = 2
HALO = 16
VMEM_LIMIT = 56 * 1024 * 1024

F32 = jnp.float32
BF16 = jnp.bfloat16


def _rms(x, g):
    return x * lax.rsqrt(jnp.mean(x * x, axis=-1, keepdims=True) + EPS) * g


def _gelu(x):
    c = math.sqrt(2.0 / math.pi)
    return x * (0.5 * (1.0 + jnp.tanh(c * (x + 0.044715 * (x * x * x)))))


def _dot(a, b):
    return jnp.dot(a, b, preferred_element_type=F32)


def _params(n_grid):
    return pltpu.CompilerParams(dimension_semantics=("parallel",) * n_grid,
                                vmem_limit_bytes=VMEM_LIMIT)


def _const_spec(shape):
    nd = len(shape)
    return pl.BlockSpec(shape, lambda *_: (0,) * nd, pipeline_mode=pl.Buffered(1))


def _ada_kernel(s_ref, w_ref, b_ref, o_ref):
    s = s_ref[...]
    s = s * jax.nn.sigmoid(s)
    o_ref[0] = _dot(s.astype(BF16), w_ref[0].astype(BF16)) + b_ref[0]


def _ada(cond, ada_w, ada_b, tn=1536):
    depth, d, n = ada_w.shape
    r = cond.shape[0]
    return pl.pallas_call(
        _ada_kernel,
        grid=(depth, n // tn),
        in_specs=[pl.BlockSpec((r, d), lambda i, j: (0, 0)),
                  pl.BlockSpec((1, d, tn), lambda i, j: (i, 0, j)),
                  pl.BlockSpec((1, 1, tn), lambda i, j: (i, 0, j))],
        out_specs=pl.BlockSpec((1, r, tn), lambda i, j: (i, 0, j)),
        out_shape=jax.ShapeDtypeStruct((depth, r, n), F32),
        compiler_params=_params(2),
        name="ada_mod",
    )(cond, ada_w, ada_b.reshape(depth, 1, n))


def _gmlp_kernel(x_ref, mod_ref, ng_ref, w_in_ref, lng_ref, lnb_ref, ws_ref, bs_ref, w_out_ref,
                 o_ref, h_ref, v_ref, t_ref, *, tile, width):
    x = x_ref[0]
    mod = mod_ref[0]
    sh, sc, gate = mod[0:1], mod[1:2], mod[2:3]
    h_ref[...] = (_rms(x, ng_ref[0:1]) * (1.0 + sc) + sh).astype(BF16)
    half = max(tile // GMLP_SPLIT, CHUNK)

    def dot_rows(lhs_ref, w):
        return jnp.concatenate([_dot(lhs_ref[r0:r0 + half], w) for r0 in range(0, tile, half)], axis=0)

    vsum = jnp.zeros((tile, 1), F32)
    for c0 in range(0, width, GMLP_CHUNK):
        vj = _gelu(dot_rows(h_ref, w_in_ref[:, width + c0:width + c0 + GMLP_CHUNK]))
        v_ref[:, c0:c0 + GMLP_CHUNK] = vj
        vsum = vsum + jnp.sum(vj, axis=-1, keepdims=True)
    mu = vsum * (1.0 / width)
    vss = jnp.zeros((tile, 1), F32)
    for c0 in range(0, width, GMLP_CHUNK):
        dv = v_ref[:, c0:c0 + GMLP_CHUNK] - mu
        vss = vss + jnp.sum(dv * dv, axis=-1, keepdims=True)
    rstd = lax.rsqrt(vss * (1.0 / width) + EPS)
    gw = width // A_GROUPS
    y = None
    for g in range(A_GROUPS):
        cols = slice(g * gw, (g + 1) * gw)
        vn = ((v_ref[:, cols] - mu) * rstd * lng_ref[:, cols] + lnb_ref[:, cols]).astype(BF16)
        u = _gelu(dot_rows(h_ref, w_in_ref[:, cols]))
        s = jnp.concatenate([_dot(ws_ref[g], vn[r0:r0 + CHUNK]) + bs_ref[g]
                             for r0 in range(0, tile, CHUNK)], axis=0)
        t_ref[:, cols] = (u * s).astype(BF16)
        if (g + 1) % GMLP_OUT_GROUPS == 0:
            k0 = (g + 1 - GMLP_OUT_GROUPS) * gw
            part = dot_rows(t_ref.at[:, k0:(g + 1) * gw], w_out_ref[k0:(g + 1) * gw, :])
            y = part if y is None else y + part
    o_ref[0] = x + gate * _rms(y, ng_ref[1:2])


def _gmlp(x, mod, ng, w_in, ln_g, ln_b, w_s, b_s, w_out, tile=512):
    bsz, n_tok, d = x.shape
    width = w_out.shape[0]
    gw = width // A_GROUPS
    tile = min(tile, n_tok)
    bs_full = jnp.broadcast_to(b_s[:, :, None], (A_GROUPS, CHUNK, gw))
    kern = functools.partial(_gmlp_kernel, tile=tile, width=width)
    return pl.pallas_call(
        kern,
        grid=(bsz, n_tok // tile),
        in_specs=[pl.BlockSpec((1, tile, d), lambda b, i: (b, i, 0)),
                  pl.BlockSpec((1, 6, d), lambda b, i: (b, 0, 0)),
                  _const_spec(ng.shape),
                  _const_spec(w_in.shape),
                  _const_spec((1, width)),
                  _const_spec((1, width)),
                  _const_spec(w_s.shape),
                  _const_spec(bs_full.shape),
                  _const_spec(w_out.shape)],
        out_specs=pl.BlockSpec((1, tile, d), lambda b, i: (b, i, 0)),
        out_shape=jax.ShapeDtypeStruct(x.shape, F32),
        scratch_shapes=[pltpu.VMEM((tile, d), BF16),
                        pltpu.VMEM((tile, width), F32),
                        pltpu.VMEM((tile, width), BF16)],
        compiler_params=_params(2),
        name="gmlp_mixer",
    )(x, mod, ng, w_in, ln_g.reshape(1, width), ln_b.reshape(1, width), w_s, bs_full, w_out)


def _ffn_kernel(x_ref, xp_ref, xn_ref, mod_ref, ng_ref, w_up_ref, cw_ref, cb_ref, w_dn_ref,
                o_ref, h_ref, *, tile, n_tiles, ffn):
    i = pl.program_id(1)
    mod = mod_ref[0]
    sh, sc, gate = mod[3:4], mod[4:5], mod[5:6]
    g_pre = ng_ref[2:3]

    def pre(xx):
        return _rms(xx, g_pre) * (1.0 + sc) + sh

    x = x_ref[0]
    has_prev = (i > 0).astype(F32)
    has_next = (i < n_tiles - 1).astype(F32)
    h_ref[0:HALO] = (pre(xp_ref[0]) * has_prev).astype(BF16)
    h_ref[HALO:HALO + tile] = pre(x).astype(BF16)
    h_ref[HALO + tile:] = (pre(xn_ref[0]) * has_next).astype(BF16)
    ext = tile + 2 * HALO
    y = None
    c0 = 0
    while c0 < ffn:
        cw = min(FFN_CHUNK, ffn - c0)
        cols = slice(c0, c0 + cw)
        cuts = [0] + [HALO + (tile * k) // FFN_SPLIT for k in range(1, FFN_SPLIT)] + [ext]
        inner = [HALO] + cuts[1:-1] + [HALO + tile]
        a_ext = jnp.concatenate([_dot(h_ref[r0:r1], w_up_ref[:, cols])
                                 for r0, r1 in zip(cuts[:-1], cuts[1:])], axis=0)
        a = (pltpu.roll(a_ext, 1, 0)[HALO:HALO + tile] * cw_ref[0:1, cols]
             + a_ext[HALO:HALO + tile] * cw_ref[1:2, cols]
             + pltpu.roll(a_ext, ext - 1, 0)[HALO:HALO + tile] * cw_ref[2:3, cols]
             + cb_ref[:, cols])
        w_b = w_up_ref[:, ffn + c0:ffn + c0 + cw]
        b = jnp.concatenate([_dot(h_ref[r0:r1], w_b) for r0, r1 in zip(inner[:-1], inner[1:])], axis=0)
        g = (_gelu(a) * b).astype(BF16)
        part = jnp.concatenate([_dot(g[r0 - HALO:r1 - HALO], w_dn_ref[cols, :])
                                for r0, r1 in zip(inner[:-1], inner[1:])], axis=0)
        y = part if y is None else y + part
        c0 += cw
    o_ref[0] = x + gate * _rms(y, ng_ref[3:4])


def _layer_spec(stacked_shape, layer):
    nd = len(stacked_shape) - 1
    return pl.BlockSpec((None,) + tuple(stacked_shape[1:]), lambda *_: (layer,) + (0,) * nd,
                        pipeline_mode=pl.Buffered(1))


def _ffn(x, mod, ng, w_up, conv_w, conv_b, w_down, layer, tile=512):
    bsz, n_tok, d = x.shape
    ffn = w_down.shape[1]
    tile = min(tile, n_tok)
    n_tiles = n_tok // tile
    hb = tile // HALO
    n_hb = n_tok // HALO
    kern = functools.partial(_ffn_kernel, tile=tile, n_tiles=n_tiles, ffn=ffn)
    return pl.pallas_call(
        kern,
        grid=(bsz, n_tiles),
        in_specs=[pl.BlockSpec((1, tile, d), lambda b, i: (b, i, 0)),
                  pl.BlockSpec((1, HALO, d), lambda b, i: (b, jnp.maximum(i * hb - 1, 0), 0)),
                  pl.BlockSpec((1, HALO, d), lambda b, i: (b, jnp.minimum((i + 1) * hb, n_hb - 1), 0)),
                  pl.BlockSpec((1, 6, d), lambda b, i: (b, 0, 0)),
                  _const_spec(ng.shape),
                  _layer_spec(w_up.shape, layer),
                  _const_spec(conv_w.shape),
                  _const_spec((1, ffn)),
                  _layer_spec(w_down.shape, layer)],
        out_specs=pl.BlockSpec((1, tile, d), lambda b, i: (b, i, 0)),
        out_shape=jax.ShapeDtypeStruct(x.shape, F32),
        scratch_shapes=[pltpu.VMEM((tile + 2 * HALO, d), BF16)],
        compiler_params=_params(2),
        name="conv_glu",
    )(x, x, x, mod, ng, w_up, conv_w, conv_b.reshape(1, ffn), w_down)


def _qkv_kernel(x_ref, mod_ref, ng_ref, w_ref, b_ref, cos_ref, sin_ref, q_ref, kt_ref, v_ref,
                *, q_dim, kv_dim):
    x = x_ref[0]
    mod = mod_ref[0]
    sh, sc = mod[0:1], mod[1:2]
    h = (_rms(x, ng_ref[0:1]) * (1.0 + sc) + sh).astype(BF16)
    half = h.shape[0] // 2
    qkv = jnp.concatenate([_dot(h[0:half], w_ref[...]), _dot(h[half:], w_ref[...])], axis=0) + b_ref[...]
    cosf = cos_ref[...]
    sinf = sin_ref[...]
    lane = lax.broadcasted_iota(jnp.int32, cosf.shape, 1)
    low_half = (lane % HEAD_DIM) < (HEAD_DIM // 2)
    scale = HEAD_DIM ** -0.5 * LOG2E
    for j in range((q_dim + kv_dim) // LANES):
        blk = qkv[:, j * LANES:(j + 1) * LANES]
        partner = jnp.where(low_half, pltpu.roll(blk, 96, 1), pltpu.roll(blk, 32, 1))
        r = blk * cosf + partner * sinf
        if j * LANES < q_dim:
            q_ref[0, :, j * LANES:(j + 1) * LANES] = (r * scale).astype(BF16)
        else:
            kt_ref[0, j * LANES - q_dim:(j + 1) * LANES - q_dim, :] = r.T.astype(BF16)
    v_ref[0] = qkv[:, q_dim + kv_dim:].astype(BF16)


def _qkv(x, mod, ng, w_qkv, b_qkv, cosf, sinf, tile=512):
    bsz, n_tok, d = x.shape
    tile = min(tile, n_tok)
    q_dim = N_HEADS * HEAD_DIM
    kv_dim = N_KV_HEADS * HEAD_DIM
    n = q_dim + 2 * kv_dim
    kern = functools.partial(_qkv_kernel, q_dim=q_dim, kv_dim=kv_dim)
    return pl.pallas_call(
        kern,
        grid=(bsz, n_tok // tile),
        in_specs=[pl.BlockSpec((1, tile, d), lambda b, i: (b, i, 0)),
                  pl.BlockSpec((1, 6, d), lambda b, i: (b, 0, 0)),
                  _const_spec(ng.shape),
                  _const_spec(w_qkv.shape),
                  _const_spec((1, n)),
                  pl.BlockSpec((tile, LANES), lambda b, i: (i, 0)),
                  pl.BlockSpec((tile, LANES), lambda b, i: (i, 0))],
        out_specs=[pl.BlockSpec((1, tile, q_dim), lambda b, i: (b, i, 0)),
                   pl.BlockSpec((1, kv_dim, tile), lambda b, i: (b, 0, i)),
                   pl.BlockSpec((1, tile, kv_dim), lambda b, i: (b, i, 0))],
        out_shape=[jax.ShapeDtypeStruct((bsz, n_tok, q_dim), BF16),
                   jax.ShapeDtypeStruct((bsz, kv_dim, n_tok), BF16),
                   jax.ShapeDtypeStruct((bsz, n_tok, kv_dim), BF16)],
        compiler_params=_params(2),
        name="qkv_rope",
    )(x, mod, ng, w_qkv, b_qkv.reshape(1, n), cosf, sinf)


def _rope_tables(n_tok):
    rows = n_tok // GRID_W
    n_freq = HEAD_DIM // 4
    row = jnp.repeat(jnp.arange(rows), GRID_W).astype(F32)
    col = jnp.tile(jnp.arange(GRID_W), rows).astype(F32)
    inv_freq = ROPE_THETA ** (-(jnp.arange(n_freq, dtype=F32) / n_freq))
    ang = jnp.concatenate([row[:, None] * inv_freq, col[:, None] * inv_freq], axis=-1)
    cos, sin = jnp.cos(ang), jnp.sin(ang)
    cosf = jnp.tile(cos, (1, 4))
    sinf = jnp.tile(jnp.concatenate([-sin, sin], axis=-1), (1, 2))
    return cosf, sinf


def _attn_kernel(sink_ref, q_ref, ktp_ref, ktc_ref, ktn_ref, vp_ref, vc_ref, vn_ref, kctx_ref, vctx_ref,
                 x_ref, mod_ref, ng_ref, wo_ref, o_ref, kwin_ref, vwin_ref, attn_ref, bias_ref, s_ref, p_ref,
                 m_ref, ve_ref, vo_ref, vce_ref, vco_ref,
                 *, tq, n_tiles):
    i = pl.program_id(1)
    nb = tq // BLOCK
    kwin_ref[:, 0:BLOCK] = ktp_ref[0]
    kwin_ref[:, BLOCK:BLOCK + tq] = ktc_ref[0]
    kwin_ref[:, BLOCK + tq:] = ktn_ref[0]
    vwin_ref[0:BLOCK] = vp_ref[0]
    vwin_ref[BLOCK:BLOCK + tq] = vc_ref[0]
    vwin_ref[BLOCK + tq:] = vn_ref[0]
    band_w = 3 * BLOCK
    r = lax.broadcasted_iota(jnp.int32, (BLOCK, band_w), 0)
    c = lax.broadcasted_iota(jnp.int32, (BLOCK, band_w), 1)
    band = (c >= r) & (c <= r + 2 * BLOCK)
    first = band & ((c >= BLOCK) | (i > 0))
    last = band & ((c < 2 * BLOCK) | (i < n_tiles - 1))
    neg = jnp.float32(-jnp.inf)
    bias_ref[0, 0] = jnp.where(first, 0.0, neg)[:, 0:BLOCK]
    bias_ref[0, 1] = jnp.where(band, 0.0, neg)[:, 0:BLOCK]
    bias_ref[1, 0] = jnp.where(last, 0.0, neg)[:, 2 * BLOCK:]
    bias_ref[1, 1] = jnp.where(band, 0.0, neg)[:, 2 * BLOCK:]

    lane = lax.broadcasted_iota(jnp.int32, (1, 2 * HEAD_DIM), 1)
    low = lane < HEAD_DIM
    one = jnp.ones((), BF16)

    def pad_values(src_ref, even_ref, odd_ref, n_rows):
        for t in range(N_KV_HEADS // 2):
            src = src_ref[0:n_rows, t * LANES:(t + 1) * LANES]
            swapped = pltpu.roll(src, HEAD_DIM, 1)
            even_ref[0:n_rows, (2 * t) * LANES:(2 * t + 1) * LANES] = jnp.where(low, src, one)
            odd_ref[0:n_rows, (2 * t) * LANES:(2 * t + 1) * LANES] = jnp.where(low, one, swapped)
            even_ref[0:n_rows, (2 * t + 1) * LANES:(2 * t + 2) * LANES] = jnp.where(low, swapped, one)
            odd_ref[0:n_rows, (2 * t + 1) * LANES:(2 * t + 2) * LANES] = jnp.where(low, one, src)

    pad_values(vwin_ref, ve_ref, vo_ref, tq + 2 * BLOCK)
    pad_values(vctx_ref.at[0], vce_ref, vco_ref, vctx_ref.shape[1])

    n_rows = Q_PER_KV * BLOCK
    half = n_rows // 2
    row_order = (0, 2, 1, 3)
    unit = 0
    for b in range(nb):
        bp = 0 if b == 0 else 1
        bn = 0 if b == nb - 1 else 1
        q = q_ref[0, b * BLOCK:(b + 1) * BLOCK, :]
        for h in range(N_KV_HEADS):
            par = unit % 2
            unit += 1
            heads = [h * Q_PER_KV + g for g in row_order]
            feat = slice(h * HEAD_DIM, (h + 1) * HEAD_DIM)
            tile = slice(h * LANES, (h + 1) * LANES)
            keys = slice(b * BLOCK, b * BLOCK + band_w)
            qh = jnp.concatenate([q[:, n * HEAD_DIM:(n + 1) * HEAD_DIM] for n in heads], axis=0)
            s_ref[par, :, 0:band_w] = _dot(qh, kwin_ref[feat, keys])
            s_ref[par, :, band_w:] = _dot(qh, kctx_ref[0, feat, :])
            for ci in range(n_rows // SM_ROWS):
                rows = slice(ci * SM_ROWS, (ci + 1) * SM_ROWS)
                r0 = (ci * SM_ROWS) % BLOCK
                sink = sink_ref[heads[ci * SM_ROWS // BLOCK]] * LOG2E
                mt = jnp.maximum(s_ref[par, rows, 0:BLOCK] + bias_ref[0, bp, r0:r0 + SM_ROWS, :],
                                 s_ref[par, rows, 2 * BLOCK:band_w] + bias_ref[1, bn, r0:r0 + SM_ROWS, :])
                for t in (1, 3, 4):
                    mt = jnp.maximum(mt, s_ref[par, rows, t * BLOCK:(t + 1) * BLOCK])
                m = jnp.maximum(jnp.max(mt, axis=-1, keepdims=True), sink)
                m_ref[par, rows, :] = jnp.broadcast_to(m, (SM_ROWS, BLOCK))
            for ci in range(n_rows // SM_ROWS):
                rows = slice(ci * SM_ROWS, (ci + 1) * SM_ROWS)
                r0 = (ci * SM_ROWS) % BLOCK
                m = m_ref[par, rows, :]
                for t in range(5):
                    st = s_ref[par, rows, t * BLOCK:(t + 1) * BLOCK]
                    if t == 0:
                        st = st + bias_ref[0, bp, r0:r0 + SM_ROWS, :]
                    if t == 2:
                        st = st + bias_ref[1, bn, r0:r0 + SM_ROWS, :]
                    p_ref[par, rows, t * BLOCK:(t + 1) * BLOCK] = jnp.exp2(st - m).astype(BF16)
            for part, (v_ref, vc_ref) in enumerate(((ve_ref, vce_ref), (vo_ref, vco_ref))):
                rows = slice(part * half, (part + 1) * half)
                acc = (_dot(p_ref[par, rows, 0:band_w], v_ref[keys, tile])
                       + _dot(p_ref[par, rows, band_w:], vc_ref[:, tile]))
                e_sink = jnp.concatenate(
                    [jnp.exp2(sink_ref[heads[part * 2 + k]] * LOG2E
                              - m_ref[par, part * half + k * BLOCK:part * half + (k + 1) * BLOCK, :])
                     for k in range(2)], axis=0)
                denom = pltpu.roll(acc, HEAD_DIM, 1) + e_sink
                o_ref_half = acc / denom
                if part == 0:
                    o_even = o_ref_half
                else:
                    o_odd = o_ref_half
            for gp in range(Q_PER_KV // 2):
                pair = jnp.where(low, o_even[gp * BLOCK:(gp + 1) * BLOCK], o_odd[gp * BLOCK:(gp + 1) * BLOCK])
                col = (h * Q_PER_KV + 2 * gp) * HEAD_DIM
                attn_ref[b * BLOCK:(b + 1) * BLOCK, col:col + 2 * HEAD_DIM] = pair.astype(BF16)
    y = _dot(attn_ref[...], wo_ref[...])
    gate = mod_ref[0][2:3]
    o_ref[0] = x_ref[0] + gate * _rms(y, ng_ref[1:2])


def _attn(x, q, kt, v, kctx_t, vctx, sink, mod, ng, w_o, tq=512):
    bsz, n_tok, d = x.shape
    n_tiles = n_tok // tq
    nb = tq // BLOCK
    n_blk = n_tok // BLOCK
    n_ctx = vctx.shape[1]
    q_dim = q.shape[2]
    kv_dim = v.shape[2]
    kern = functools.partial(_attn_kernel, tq=tq, n_tiles=n_tiles)
    cur_map = lambda b, i: (b, i, 0)
    batch_map = lambda b, i: (b, 0, 0)
    return pl.pallas_call(
        kern,
        grid=(bsz, n_tiles),
        in_specs=[pl.BlockSpec(memory_space=pltpu.SMEM),
                  pl.BlockSpec((1, tq, q_dim), cur_map),
                  pl.BlockSpec((1, kv_dim, BLOCK), lambda b, i: (b, 0, jnp.maximum(i * nb - 1, 0))),
                  pl.BlockSpec((1, kv_dim, tq), lambda b, i: (b, 0, i)),
                  pl.BlockSpec((1, kv_dim, BLOCK), lambda b, i: (b, 0, jnp.minimum((i + 1) * nb, n_blk - 1))),
                  pl.BlockSpec((1, BLOCK, kv_dim), lambda b, i: (b, jnp.maximum(i * nb - 1, 0), 0)),
                  pl.BlockSpec((1, tq, kv_dim), cur_map),
                  pl.BlockSpec((1, BLOCK, kv_dim), lambda b, i: (b, jnp.minimum((i + 1) * nb, n_blk - 1), 0)),
                  pl.BlockSpec((1, kv_dim, n_ctx), batch_map),
                  pl.BlockSpec((1, n_ctx, kv_dim), batch_map),
                  pl.BlockSpec((1, tq, d), cur_map),
                  pl.BlockSpec((1, 6, d), batch_map),
                  _const_spec(ng.shape),
                  _const_spec(w_o.shape)],
        out_specs=pl.BlockSpec((1, tq, d), cur_map),
        out_shape=jax.ShapeDtypeStruct(x.shape, F32),
        scratch_shapes=[pltpu.VMEM((kv_dim, tq + 2 * BLOCK), BF16),
                        pltpu.VMEM((tq + 2 * BLOCK, kv_dim), BF16),
                        pltpu.VMEM((tq, q_dim), BF16),
                        pltpu.VMEM((2, 2, BLOCK, BLOCK), F32),
                        pltpu.VMEM((2, Q_PER_KV * BLOCK, 3 * BLOCK + n_ctx), F32),
                        pltpu.VMEM((2, Q_PER_KV * BLOCK, 3 * BLOCK + n_ctx), BF16),
                        pltpu.VMEM((2, Q_PER_KV * BLOCK, BLOCK), F32),
                        pltpu.VMEM((tq + 2 * BLOCK, N_KV_HEADS * LANES), BF16),
                        pltpu.VMEM((tq + 2 * BLOCK, N_KV_HEADS * LANES), BF16),
                        pltpu.VMEM((n_ctx, N_KV_HEADS * LANES), BF16),
                        pltpu.VMEM((n_ctx, N_KV_HEADS * LANES), BF16)],
        compiler_params=_params(2),
        name="window_attn",
    )(sink, q, kt, kt, kt, v, v, v, kctx_t, vctx, x, mod, ng, w_o)


def kernel(x, c, ctx, c_ctx, ada_w, ada_b, norm_g, a_w_in, a_ln_g, a_ln_b, a_w_s, a_b_s, a_w_out,
           b_w_qkv, b_b_qkv, b_sink, b_w_o, f_w_up, f_conv_w, f_conv_b, f_w_down):
    bsz, n_tok, d = x.shape
    n_ctx = ctx.shape[1]
    depth = ada_w.shape[0]
    assert depth == 2, "layer 0 = gMLP mixer, layer 1 = windowed attention (last layer: no context update)"

    rows = 8 * ((bsz + 1 + 7) // 8)
    cond = jnp.zeros((rows, d), F32).at[:bsz].set(c).at[bsz].set(c_ctx)
    mods = _ada(cond, ada_w, ada_b)

    def lat_mod(i):
        return mods[i, :bsz].reshape(bsz, 6, d)

    def ctx_mod(i):
        return jnp.broadcast_to(mods[i, bsz].reshape(1, 6, d), (bsz, 6, d))

    def ctx_mod_flat(i):
        return mods[i, bsz].reshape(1, 6, d)

    bf = lambda w: w.astype(BF16)
    w_up, w_down = bf(f_w_up), bf(f_w_down)

    ctx_flat = lambda a: a.reshape(1, bsz * n_ctx, a.shape[-1])

    gm = (norm_g[0], bf(a_w_in[0]), a_ln_g[0], a_ln_b[0], bf(a_w_s[0]), a_b_s[0], bf(a_w_out[0]))
    ff0 = (norm_g[0], w_up, f_conv_w[0], f_conv_b[0], w_down, 0)
    x = _gmlp(x, lat_mod(0), *gm)
    ctx = _gmlp(ctx_flat(ctx), ctx_mod_flat(0), *gm).reshape(bsz, n_ctx, d)
    x = _ffn(x, lat_mod(0), *ff0)
    ctx = _ffn(ctx, ctx_mod(0), *ff0)

    cosf, sinf = _rope_tables(n_tok)
    w_qkv = bf(b_w_qkv[0])
    q, k, v = _qkv(x, lat_mod(1), norm_g[1], w_qkv, b_b_qkv[0], cosf, sinf)
    _, kctx, vctx = _qkv(ctx_flat(ctx), ctx_mod_flat(1), norm_g[1], w_qkv, b_b_qkv[0],
                         jnp.ones((bsz * n_ctx, LANES), F32), jnp.zeros((bsz * n_ctx, LANES), F32))
    kv_dim = kctx.shape[1]
    kctx = kctx.reshape(kv_dim, bsz, n_ctx).transpose(1, 0, 2)
    vctx = vctx.reshape(bsz, n_ctx, kv_dim)
    x = _attn(x, q, k, v, kctx, vctx, b_sink[0], lat_mod(1), norm_g[1], bf(b_w_o[0]))
    x = _ffn(x, lat_mod(1), norm_g[1], w_up, f_conv_w[1], f_conv_b[1], w_down, 1)
    return x
```

```python
import functools
import math

import jax
import jax.numpy as jnp
from jax import lax
from jax.experimental import pallas as pl
from jax.experimental.pallas import tpu as pltpu

EPS = 1e-6
GRID_W = 64
CHUNK = 128
A_GROUPS = 8
HEAD_DIM = 64
N_HEADS = 16
N_KV_HEADS = 4
Q_PER_KV = N_HEADS // N_KV_HEADS
BLOCK = 128
ROPE_THETA = 10000.0
LANES = 128
LOG2E = math.log2(math.e)
SM_ROWS = 32
GMLP_CHUNK = 512
GMLP_OUT_GROUPS = 8
GMLP_SPLIT = 2
FFN_CHUNK = 2816
FFN_SPLIT = 2
HALO = 16
VMEM_LIMIT = 56 * 1024 * 1024

F32 = jnp.float32
BF16 = jnp.bfloat16


def _rms(x, g):
    return x * lax.rsqrt(jnp.mean(x * x, axis=-1, keepdims=True) + EPS) * g


def _gelu(x):
    c = math.sqrt(2.0 / math.pi)
    return x * (0.5 * (1.0 + jnp.tanh(c * (x + 0.044715 * (x * x * x)))))


def _dot(a, b):
    return jnp.dot(a, b, preferred_element_type=F32)


def _params(n_grid):
    return pltpu.CompilerParams(dimension_semantics=("parallel",) * n_grid,
                                vmem_limit_bytes=VMEM_LIMIT)


def _const_spec(shape):
    nd = len(shape)
    return pl.BlockSpec(shape, lambda *_: (0,) * nd, pipeline_mode=pl.Buffered(1))


def _ada_kernel(s_ref, w_ref, b_ref, o_ref):
    s = s_ref[...]
    s = s * jax.nn.sigmoid(s)
    o_ref[0] = _dot(s.astype(BF16), w_ref[0].astype(BF16)) + b_ref[0]


def _ada(cond, ada_w, ada_b, tn=1536):
    depth, d, n = ada_w.shape
    r = cond.shape[0]
    return pl.pallas_call(
        _ada_kernel,
        grid=(depth, n // tn),
        in_specs=[pl.BlockSpec((r, d), lambda i, j: (0, 0)),
                  pl.BlockSpec((1, d, tn), lambda i, j: (i, 0, j)),
                  pl.BlockSpec((1, 1, tn), lambda i, j: (i, 0, j))],
        out_specs=pl.BlockSpec((1, r, tn), lambda i, j: (i, 0, j)),
        out_shape=jax.ShapeDtypeStruct((depth, r, n), F32),
        compiler_params=_params(2),
        name="ada_mod",
    )(cond, ada_w, ada_b.reshape(depth, 1, n))


def _gmlp_kernel(x_ref, mod_ref, ng_ref, w_in_ref, lng_ref, lnb_ref, ws_ref, bs_ref, w_out_ref,
                 o_ref, h_ref, v_ref, t_ref, *, tile, width):
    x = x_ref[0]
    mod = mod_ref[0]
    sh, sc, gate = mod[0:1], mod[1:2], mod[2:3]
    h_ref[...] = (_rms(x, ng_ref[0:1]) * (1.0 + sc) + sh).astype(BF16)
    half = max(tile // GMLP_SPLIT, CHUNK)

    def dot_rows(lhs_ref, w):
        return jnp.concatenate([_dot(lhs_ref[r0:r0 + half], w) for r0 in range(0, tile, half)], axis=0)

    vsum = jnp.zeros((tile, 1), F32)
    for c0 in range(0, width, GMLP_CHUNK):
        vj = _gelu(dot_rows(h_ref, w_in_ref[:, width + c0:width + c0 + GMLP_CHUNK]))
        v_ref[:, c0:c0 + GMLP_CHUNK] = vj
        vsum = vsum + jnp.sum(vj, axis=-1, keepdims=True)
    mu = vsum * (1.0 / width)
    vss = jnp.zeros((tile, 1), F32)
    for c0 in range(0, width, GMLP_CHUNK):
        dv = v_ref[:, c0:c0 + GMLP_CHUNK] - mu
        vss = vss + jnp.sum(dv * dv, axis=-1, keepdims=True)
    rstd = lax.rsqrt(vss * (1.0 / width) + EPS)
    gw = width // A_GROUPS
    y = None
    for g in range(A_GROUPS):
        cols = slice(g * gw, (g + 1) * gw)
        vn = ((v_ref[:, cols] - mu) * rstd * lng_ref[:, cols] + lnb_ref[:, cols]).astype(BF16)
        u = _gelu(dot_rows(h_ref, w_in_ref[:, cols]))
        s = jnp.concatenate([_dot(ws_ref[g], vn[r0:r0 + CHUNK]) + bs_ref[g]
                             for r0 in range(0, tile, CHUNK)], axis=0)
        t_ref[:, cols] = (u * s).astype(BF16)
        if (g + 1) % GMLP_OUT_GROUPS == 0:
            k0 = (g + 1 - GMLP_OUT_GROUPS) * gw
            part = dot_rows(t_ref.at[:, k0:(g + 1) * gw], w_out_ref[k0:(g + 1) * gw, :])
            y = part if y is None else y + part
    o_ref[0] = x + gate * _rms(y, ng_ref[1:2])


def _gmlp(x, mod, ng, w_in, ln_g, ln_b, w_s, b_s, w_out, tile=512):
    bsz, n_tok, d = x.shape
    width = w_out.shape[0]
    gw = width // A_GROUPS
    tile = min(tile, n_tok)
    bs_full = jnp.broadcast_to(b_s[:, :, None], (A_GROUPS, CHUNK, gw))
    kern = functools.partial(_gmlp_kernel, tile=tile, width=width)
    return pl.pallas_call(
        kern,
        grid=(bsz, n_tok // tile),
        in_specs=[pl.BlockSpec((1, tile, d), lambda b, i: (b, i, 0)),
                  pl.BlockSpec((1, 6, d), lambda b, i: (b, 0, 0)),
                  _const_spec(ng.shape),
                  _const_spec(w_in.shape),
                  _const_spec((1, width)),
                  _const_spec((1, width)),
                  _const_spec(w_s.shape),
                  _const_spec(bs_full.shape),
                  _const_spec(w_out.shape)],
        out_specs=pl.BlockSpec((1, tile, d), lambda b, i: (b, i, 0)),
        out_shape=jax.ShapeDtypeStruct(x.shape, F32),
        scratch_shapes=[pltpu.VMEM((tile, d), BF16),
                        pltpu.VMEM((tile, width), F32),
                        pltpu.VMEM((tile, width), BF16)],
        compiler_params=_params(2),
        name="gmlp_mixer",
    )(x, mod, ng, w_in, ln_g.reshape(1, width), ln_b.reshape(1, width), w_s, bs_full, w_out)


def _ffn_kernel(x_ref, xp_ref, xn_ref, mod_ref, ng_ref, w_up_ref, cw_ref, cb_ref, w_dn_ref,
                o_ref, h_ref, *, tile, n_tiles, ffn):
    i = pl.program_id(1)
    mod = mod_ref[0]
    sh, sc, gate = mod[3:4], mod[4:5], mod[5:6]
    g_pre = ng_ref[2:3]

    def pre(xx):
        return _rms(xx, g_pre) * (1.0 + sc) + sh

    x = x_ref[0]
    has_prev = (i > 0).astype(F32)
    has_next = (i < n_tiles - 1).astype(F32)
    h_ref[0:HALO] = (pre(xp_ref[0]) * has_prev).astype(BF16)
    h_ref[HALO:HALO + tile] = pre(x).astype(BF16)
    h_ref[HALO + tile:] = (pre(xn_ref[0]) * has_next).astype(BF16)
    ext = tile + 2 * HALO
    y = None
    c0 = 0
    while c0 < ffn:
        cw = min(FFN_CHUNK, ffn - c0)
        cols = slice(c0, c0 + cw)
        cuts = [0] + [HALO + (tile * k) // FFN_SPLIT for k in range(1, FFN_SPLIT)] + [ext]
        inner = [HALO] + cuts[1:-1] + [HALO + tile]
        a_ext = jnp.concatenate([_dot(h_ref[r0:r1], w_up_ref[:, cols])
                                 for r0, r1 in zip(cuts[:-1], cuts[1:])], axis=0)
        a = (pltpu.roll(a_ext, 1, 0)[HALO:HALO + tile] * cw_ref[0:1, cols]
             + a_ext[HALO:HALO + tile] * cw_ref[1:2, cols]
             + pltpu.roll(a_ext, ext - 1, 0)[HALO:HALO + tile] * cw_ref[2:3, cols]
             + cb_ref[:, cols])
        w_b = w_up_ref[:, ffn + c0:ffn + c0 + cw]
        b = jnp.concatenate([_dot(h_ref[r0:r1], w_b) for r0, r1 in zip(inner[:-1], inner[1:])], axis=0)
        g = (_gelu(a) * b).astype(BF16)
        part = jnp.concatenate([_dot(g[r0 - HALO:r1 - HALO], w_dn_ref[cols, :])
                                for r0, r1 in zip(inner[:-1], inner[1:])], axis=0)
        y = part if y is None else y + part
        c0 += cw
    o_ref[0] = x + gate * _rms(y, ng_ref[3:4])


def _layer_spec(stacked_shape, layer):
    nd = len(stacked_shape) - 1
    return pl.BlockSpec((None,) + tuple(stacked_shape[1:]), lambda *_: (layer,) + (0,) * nd,
                        pipeline_mode=pl.Buffered(1))


def _ffn(x, mod, ng, w_up, conv_w, conv_b, w_down, layer, tile=512):
    bsz, n_tok, d = x.shape
    ffn = w_down.shape[1]
    tile = min(tile, n_tok)
    n_tiles = n_tok // tile
    hb = tile // HALO
    n_hb = n_tok // HALO
    kern = functools.partial(_ffn_kernel, tile=tile, n_tiles=n_tiles, ffn=ffn)
    return pl.pallas_call(
        kern,
        grid=(bsz, n_tiles),
        in_specs=[pl.BlockSpec((1, tile, d), lambda b, i: (b, i, 0)),
                  pl.BlockSpec((1, HALO, d), lambda b, i: (b, jnp.maximum(i * hb - 1, 0), 0)),
                  pl.BlockSpec((1, HALO, d), lambda b, i: (b, jnp.minimum((i + 1) * hb, n_hb - 1), 0)),
                  pl.BlockSpec((1, 6, d), lambda b, i: (b, 0, 0)),
                  _const_spec(ng.shape),
                  _layer_spec(w_up.shape, layer),
                  _const_spec(conv_w.shape),
                  _const_spec((1, ffn)),
                  _layer_spec(w_down.shape, layer)],
        out_specs=pl.BlockSpec((1, tile, d), lambda b, i: (b, i, 0)),
        out_shape=jax.ShapeDtypeStruct(x.shape, F32),
        scratch_shapes=[pltpu.VMEM((tile + 2 * HALO, d), BF16)],
        compiler_params=_params(2),
        name="conv_glu",
    )(x, x, x, mod, ng, w_up, conv_w, conv_b.reshape(1, ffn), w_down)


def _qkv_kernel(x_ref, mod_ref, ng_ref, w_ref, b_ref, cos_ref, sin_ref, q_ref, kt_ref, v_ref,
                *, q_dim, kv_dim):
    x = x_ref[0]
    mod = mod_ref[0]
    sh, sc = mod[0:1], mod[1:2]
    h = (_rms(x, ng_ref[0:1]) * (1.0 + sc) + sh).astype(BF16)
    half = h.shape[0] // 2
    qkv = jnp.concatenate([_dot(h[0:half], w_ref[...]), _dot(h[half:], w_ref[...])], axis=0) + b_ref[...]
    cosf = cos_ref[...]
    sinf = sin_ref[...]
    lane = lax.broadcasted_iota(jnp.int32, cosf.shape, 1)
    low_half = (lane % HEAD_DIM) < (HEAD_DIM // 2)
    scale = HEAD_DIM ** -0.5 * LOG2E
    for j in range((q_dim + kv_dim) // LANES):
        blk = qkv[:, j * LANES:(j + 1) * LANES]
        partner = jnp.where(low_half, pltpu.roll(blk, 96, 1), pltpu.roll(blk, 32, 1))
        r = blk * cosf + partner * sinf
        if j * LANES < q_dim:
            q_ref[0, :, j * LANES:(j + 1) * LANES] = (r * scale).astype(BF16)
        else:
            kt_ref[0, j * LANES - q_dim:(j + 1) * LANES - q_dim, :] = r.T.astype(BF16)
    v_ref[0] = qkv[:, q_dim + kv_dim:].astype(BF16)


def _qkv(x, mod, ng, w_qkv, b_qkv, cosf, sinf, tile=1024):
    bsz, n_tok, d = x.shape
    tile = min(tile, n_tok)
    q_dim = N_HEADS * HEAD_DIM
    kv_dim = N_KV_HEADS * HEAD_DIM
    n = q_dim + 2 * kv_dim
    kern = functools.partial(_qkv_kernel, q_dim=q_dim, kv_dim=kv_dim)
    return pl.pallas_call(
        kern,
        grid=(bsz, n_tok // tile),
        in_specs=[pl.BlockSpec((1, tile, d), lambda b, i: (b, i, 0)),
                  pl.BlockSpec((1, 6, d), lambda b, i: (b, 0, 0)),
                  _const_spec(ng.shape),
                  _const_spec(w_qkv.shape),
                  _const_spec((1, n)),
                  pl.BlockSpec((tile, LANES), lambda b, i: (i, 0)),
                  pl.BlockSpec((tile, LANES), lambda b, i: (i, 0))],
        out_specs=[pl.BlockSpec((1, tile, q_dim), lambda b, i: (b, i, 0)),
                   pl.BlockSpec((1, kv_dim, tile), lambda b, i: (b, 0, i)),
                   pl.BlockSpec((1, tile, kv_dim), lambda b, i: (b, i, 0))],
        out_shape=[jax.ShapeDtypeStruct((bsz, n_tok, q_dim), BF16),
                   jax.ShapeDtypeStruct((bsz, kv_dim, n_tok), BF16),
                   jax.ShapeDtypeStruct((bsz, n_tok, kv_dim), BF16)],
        compiler_params=_params(2),
        name="qkv_rope",
    )(x, mod, ng, w_qkv, b_qkv.reshape(1, n), cosf, sinf)


def _rope_tables(n_tok):
    rows = n_tok // GRID_W
    n_freq = HEAD_DIM // 4
    row = jnp.repeat(jnp.arange(rows), GRID_W).astype(F32)
    col = jnp.tile(jnp.arange(GRID_W), rows).astype(F32)
    inv_freq = ROPE_THETA ** (-(jnp.arange(n_freq, dtype=F32) / n_freq))
    ang = jnp.concatenate([row[:, None] * inv_freq, col[:, None] * inv_freq], axis=-1)
    cos, sin = jnp.cos(ang), jnp.sin(ang)
    cosf = jnp.tile(cos, (1, 4))
    sinf = jnp.tile(jnp.concatenate([-sin, sin], axis=-1), (1, 2))
    return cosf, sinf


def _attn_kernel(sink_ref, q_ref, ktp_ref, ktc_ref, ktn_ref, vp_ref, vc_ref, vn_ref, kctx_ref, vctx_ref,
                 x_ref, mod_ref, ng_ref, wo_ref, o_ref, kwin_ref, vwin_ref, attn_ref, bias_ref, s_ref, p_ref,
                 m_ref, ve_ref, vo_ref, vce_ref, vco_ref,
                 *, tq, n_tiles):
    i = pl.program_id(1)
    nb = tq // BLOCK
    kwin_ref[:, 0:BLOCK] = ktp_ref[0]
    kwin_ref[:, BLOCK:BLOCK + tq] = ktc_ref[0]
    kwin_ref[:, BLOCK + tq:] = ktn_ref[0]
    vwin_ref[0:BLOCK] = vp_ref[0]
    vwin_ref[BLOCK:BLOCK + tq] = vc_ref[0]
    vwin_ref[BLOCK + tq:] = vn_ref[0]
    band_w = 3 * BLOCK
    r = lax.broadcasted_iota(jnp.int32, (BLOCK, band_w), 0)
    c = lax.broadcasted_iota(jnp.int32, (BLOCK, band_w), 1)
    band = (c >= r) & (c <= r + 2 * BLOCK)
    first = band & ((c >= BLOCK) | (i > 0))
    last = band & ((c < 2 * BLOCK) | (i < n_tiles - 1))
    neg = jnp.float32(-jnp.inf)
    bias_ref[0, 0] = jnp.where(first, 0.0, neg)[:, 0:BLOCK]
    bias_ref[0, 1] = jnp.where(band, 0.0, neg)[:, 0:BLOCK]
    bias_ref[1, 0] = jnp.where(last, 0.0, neg)[:, 2 * BLOCK:]
    bias_ref[1, 1] = jnp.where(band, 0.0, neg)[:, 2 * BLOCK:]

    lane = lax.broadcasted_iota(jnp.int32, (1, 2 * HEAD_DIM), 1)
    low = lane < HEAD_DIM
    one = jnp.ones((), BF16)

    def pad_values(src_ref, even_ref, odd_ref, n_rows):
        for t in range(N_KV_HEADS // 2):
            src = src_ref[0:n_rows, t * LANES:(t + 1) * LANES]
            swapped = pltpu.roll(src, HEAD_DIM, 1)
            even_ref[0:n_rows, (2 * t) * LANES:(2 * t + 1) * LANES] = jnp.where(low, src, one)
            odd_ref[0:n_rows, (2 * t) * LANES:(2 * t + 1) * LANES] = jnp.where(low, one, swapped)
            even_ref[0:n_rows, (2 * t + 1) * LANES:(2 * t + 2) * LANES] = jnp.where(low, swapped, one)
            odd_ref[0:n_rows, (2 * t + 1) * LANES:(2 * t + 2) * LANES] = jnp.where(low, one, src)

    pad_values(vwin_ref, ve_ref, vo_ref, tq + 2 * BLOCK)
    pad_values(vctx_ref.at[0], vce_ref, vco_ref, vctx_ref.shape[1])

    n_rows = Q_PER_KV * BLOCK
    half = n_rows // 2
    row_order = (0, 2, 1, 3)
    unit = 0
    for b in range(nb):
        bp = 0 if b == 0 else 1
        bn = 0 if b == nb - 1 else 1
        q = q_ref[0, b * BLOCK:(b + 1) * BLOCK, :]
        for h in range(N_KV_HEADS):
            par = unit % 2
            unit += 1
            heads = [h * Q_PER_KV + g for g in row_order]
            feat = slice(h * HEAD_DIM, (h + 1) * HEAD_DIM)
            tile = slice(h * LANES, (h + 1) * LANES)
            keys = slice(b * BLOCK, b * BLOCK + band_w)
            qh = jnp.concatenate([q[:, n * HEAD_DIM:(n + 1) * HEAD_DIM] for n in heads], axis=0)
            s_ref[par, :, 0:band_w] = _dot(qh, kwin_ref[feat, keys])
            s_ref[par, :, band_w:] = _dot(qh, kctx_ref[0, feat, :])
            for ci in range(n_rows // SM_ROWS):
                rows = slice(ci * SM_ROWS, (ci + 1) * SM_ROWS)
                r0 = (ci * SM_ROWS) % BLOCK
                sink = sink_ref[heads[ci * SM_ROWS // BLOCK]] * LOG2E
                mt = jnp.maximum(s_ref[par, rows, 0:BLOCK] + bias_ref[0, bp, r0:r0 + SM_ROWS, :],
                                 s_ref[par, rows, 2 * BLOCK:band_w] + bias_ref[1, bn, r0:r0 + SM_ROWS, :])
                for t in (1, 3, 4):
                    mt = jnp.maximum(mt, s_ref[par, rows, t * BLOCK:(t + 1) * BLOCK])
                m = jnp.maximum(jnp.max(mt, axis=-1, keepdims=True), sink)
                m_ref[par, rows, :] = jnp.broadcast_to(m, (SM_ROWS, BLOCK))
            for ci in range(n_rows // SM_ROWS):
                rows = slice(ci * SM_ROWS, (ci + 1) * SM_ROWS)
                r0 = (ci * SM_ROWS) % BLOCK
                m = m_ref[par, rows, :]
                for t in range(5):
                    st = s_ref[par, rows, t * BLOCK:(t + 1) * BLOCK]
                    if t == 0:
                        st = st + bias_ref[0, bp, r0:r0 + SM_ROWS, :]
                    if t == 2:
                        st = st + bias_ref[1, bn, r0:r0 + SM_ROWS, :]
                    p_ref[par, rows, t * BLOCK:(t + 1) * BLOCK] = jnp.exp2(st - m).astype(BF16)
            for part, (v_ref, vc_ref) in enumerate(((ve_ref, vce_ref), (vo_ref, vco_ref))):
                rows = slice(part * half, (part + 1) * half)
                acc = (_dot(p_ref[par, rows, 0:band_w], v_ref[keys, tile])
                       + _dot(p_ref[par, rows, band_w:], vc_ref[:, tile]))
                e_sink = jnp.concatenate(
                    [jnp.exp2(sink_ref[heads[part * 2 + k]] * LOG2E
                              - m_ref[par, part * half + k * BLOCK:part * half + (k + 1) * BLOCK, :])
                     for k in range(2)], axis=0)
                denom = pltpu.roll(acc, HEAD_DIM, 1) + e_sink
                o_ref_half = acc / denom
                if part == 0:
                    o_even = o_ref_half
                else:
                    o_odd = o_ref_half
            for gp in range(Q_PER_KV // 2):
                pair = jnp.where(low, o_even[gp * BLOCK:(gp + 1) * BLOCK], o_odd[gp * BLOCK:(gp + 1) * BLOCK])
                col = (h * Q_PER_KV + 2 * gp) * HEAD_DIM
                attn_ref[b * BLOCK:(b + 1) * BLOCK, col:col + 2 * HEAD_DIM] = pair.astype(BF16)
    y = _dot(attn_ref[...], wo_ref[...])
    gate = mod_ref[0][2:3]
    o_ref[0] = x_ref[0] + gate * _rms(y, ng_ref[1:2])


def _attn(x, q, kt, v, kctx_t, vctx, sink, mod, ng, w_o, tq=512):
    bsz, n_tok, d = x.shape
    n_tiles = n_tok // tq
    nb = tq // BLOCK
    n_blk = n_tok // BLOCK
    n_ctx = vctx.shape[1]
    q_dim = q.shape[2]
    kv_dim = v.shape[2]
    kern = functools.partial(_attn_kernel, tq=tq, n_tiles=n_tiles)
    cur_map = lambda b, i: (b, i, 0)
    batch_map = lambda b, i: (b, 0, 0)
    return pl.pallas_call(
        kern,
        grid=(bsz, n_tiles),
        in_specs=[pl.BlockSpec(memory_space=pltpu.SMEM),
                  pl.BlockSpec((1, tq, q_dim), cur_map),
                  pl.BlockSpec((1, kv_dim, BLOCK), lambda b, i: (b, 0, jnp.maximum(i * nb - 1, 0))),
                  pl.BlockSpec((1, kv_dim, tq), lambda b, i: (b, 0, i)),
                  pl.BlockSpec((1, kv_dim, BLOCK), lambda b, i: (b, 0, jnp.minimum((i + 1) * nb, n_blk - 1))),
                  pl.BlockSpec((1, BLOCK, kv_dim), lambda b, i: (b, jnp.maximum(i * nb - 1, 0), 0)),
                  pl.BlockSpec((1, tq, kv_dim), cur_map),
                  pl.BlockSpec((1, BLOCK, kv_dim), lambda b, i: (b, jnp.minimum((i + 1) * nb, n_blk - 1), 0)),
                  pl.BlockSpec((1, kv_dim, n_ctx), batch_map),
                  pl.BlockSpec((1, n_ctx, kv_dim), batch_map),
                  pl.BlockSpec((1, tq, d), cur_map),
                  pl.BlockSpec((1, 6, d), batch_map),
                  _const_spec(ng.shape),
                  _const_spec(w_o.shape)],
        out_specs=pl.BlockSpec((1, tq, d), cur_map),
        out_shape=jax.ShapeDtypeStruct(x.shape, F32),
        scratch_shapes=[pltpu.VMEM((kv_dim, tq + 2 * BLOCK), BF16),
                        pltpu.VMEM((tq + 2 * BLOCK, kv_dim), BF16),
                        pltpu.VMEM((tq, q_dim), BF16),
                        pltpu.VMEM((2, 2, BLOCK, BLOCK), F32),
                        pltpu.VMEM((2, Q_PER_KV * BLOCK, 3 * BLOCK + n_ctx), F32),
                        pltpu.VMEM((2, Q_PER_KV * BLOCK, 3 * BLOCK + n_ctx), BF16),
                        pltpu.VMEM((2, Q_PER_KV * BLOCK, BLOCK), F32),
                        pltpu.VMEM((tq + 2 * BLOCK, N_KV_HEADS * LANES), BF16),
                        pltpu.VMEM((tq + 2 * BLOCK, N_KV_HEADS * LANES), BF16),
                        pltpu.VMEM((n_ctx, N_KV_HEADS * LANES), BF16),
                        pltpu.VMEM((n_ctx, N_KV_HEADS * LANES), BF16)],
        compiler_params=_params(2),
        name="window_attn",
    )(sink, q, kt, kt, kt, v, v, v, kctx_t, vctx, x, mod, ng, w_o)


def kernel(x, c, ctx, c_ctx, ada_w, ada_b, norm_g, a_w_in, a_ln_g, a_ln_b, a_w_s, a_b_s, a_w_out,
           b_w_qkv, b_b_qkv, b_sink, b_w_o, f_w_up, f_conv_w, f_conv_b, f_w_down):
    bsz, n_tok, d = x.shape
    n_ctx = ctx.shape[1]
    depth = ada_w.shape[0]
    assert depth == 2, "layer 0 = gMLP mixer, layer 1 = windowed attention (last layer: no context update)"

    rows = 8 * ((bsz + 1 + 7) // 8)
    cond = jnp.zeros((rows, d), F32).at[:bsz].set(c).at[bsz].set(c_ctx)
    mods = _ada(cond, ada_w, ada_b)

    def lat_mod(i):
        return mods[i, :bsz].reshape(bsz, 6, d)

    def ctx_mod(i):
        return jnp.broadcast_to(mods[i, bsz].reshape(1, 6, d), (bsz, 6, d))

    def ctx_mod_flat(i):
        return mods[i, bsz].reshape(1, 6, d)

    bf = lambda w: w.astype(BF16)
    w_up, w_down = bf(f_w_up), bf(f_w_down)

    ctx_flat = lambda a: a.reshape(1, bsz * n_ctx, a.shape[-1])

    gm = (norm_g[0], bf(a_w_in[0]), a_ln_g[0], a_ln_b[0], bf(a_w_s[0]), a_b_s[0], bf(a_w_out[0]))
    ff0 = (norm_g[0], w_up, f_conv_w[0], f_conv_b[0], w_down, 0)
    x = _gmlp(x, lat_mod(0), *gm)
    ctx = _gmlp(ctx_flat(ctx), ctx_mod_flat(0), *gm).reshape(bsz, n_ctx, d)
    x = _ffn(x, lat_mod(0), *ff0)
    ctx = _ffn(ctx, ctx_mod(0), *ff0)

    cosf, sinf = _rope_tables(n_tok)
    w_qkv = bf(b_w_qkv[0])
    q, k, v = _qkv(x, lat_mod(1), norm_g[1], w_qkv, b_b_qkv[0], cosf, sinf)
    _, kctx, vctx = _qkv(ctx_flat(ctx), ctx_mod_flat(1), norm_g[1], w_qkv, b_b_qkv[0],
                         jnp.ones((bsz * n_ctx, LANES), F32), jnp.zeros((bsz * n_ctx, LANES), F32))
    kv_dim = kctx.shape[1]
    kctx = kctx.reshape(kv_dim, bsz, n_ctx).transpose(1, 0, 2)
    vctx = vctx.reshape(bsz, n_ctx, kv_dim)
    x = _attn(x, q, k, v, kctx, vctx, b_sink[0], lat_mod(1), norm_g[1], bf(b_w_o[0]))
    x = _ffn(x, lat_mod(1), norm_g[1], w_up, f_conv_w[1], f_conv_b[1], w_down, 1)
    return x
```

```python
import functools
import math

import jax
import jax.numpy as jnp
from jax import lax
from jax.experimental import pallas as pl
from jax.experimental.pallas import tpu as pltpu

EPS = 1e-6
GRID_W = 64
CHUNK = 128
A_GROUPS = 8
HEAD_DIM = 64
N_HEADS = 16
N_KV_HEADS = 4
Q_PER_KV = N_HEADS // N_KV_HEADS
BLOCK = 128
ROPE_THETA = 10000.0
LANES = 128
LOG2E = math.log2(math.e)
SM_ROWS = 32
GMLP_CHUNK = 512
GMLP_OUT_GROUPS = 8
GMLP_SPLIT = 2
FFN_CHUNK = 2816
FFN_SPLIT = 2
HALO = 16
VMEM_LIMIT = 56 * 1024 * 1024

F32 = jnp.float32
BF16 = jnp.bfloat16


def _rms(x, g):
    return x * lax.rsqrt(jnp.mean(x * x, axis=-1, keepdims=True) + EPS) * g


def _gelu(x):
    c = math.sqrt(2.0 / math.pi)
    return x * (0.5 * (1.0 + jnp.tanh(c * (x + 0.044715 * (x * x * x)))))


def _dot(a, b):
    return jnp.dot(a, b, preferred_element_type=F32)


def _params(n_grid):
    return pltpu.CompilerParams(dimension_semantics=("parallel",) * n_grid,
                                vmem_limit_bytes=VMEM_LIMIT)


def _const_spec(shape):
    nd = len(shape)
    return pl.BlockSpec(shape, lambda *_: (0,) * nd, pipeline_mode=pl.Buffered(1))


def _ada_kernel(s_ref, w_ref, b_ref, o_ref):
    s = s_ref[...]
    s = s * jax.nn.sigmoid(s)
    o_ref[0] = _dot(s.astype(BF16), w_ref[0].astype(BF16)) + b_ref[0]


def _ada(cond, ada_w, ada_b, tn=1536):
    depth, d, n = ada_w.shape
    r = cond.shape[0]
    return pl.pallas_call(
        _ada_kernel,
        grid=(depth, n // tn),
        in_specs=[pl.BlockSpec((r, d), lambda i, j: (0, 0)),
                  pl.BlockSpec((1, d, tn), lambda i, j: (i, 0, j)),
                  pl.BlockSpec((1, 1, tn), lambda i, j: (i, 0, j))],
        out_specs=pl.BlockSpec((1, r, tn), lambda i, j: (i, 0, j)),
        out_shape=jax.ShapeDtypeStruct((depth, r, n), F32),
        compiler_params=_params(2),
        name="ada_mod",
    )(cond, ada_w, ada_b.reshape(depth, 1, n))


def _gmlp_kernel(x_ref, mod_ref, ng_ref, w_in_ref, lng_ref, lnb_ref, ws_ref, bs_ref, w_out_ref,
                 o_ref, h_ref, v_ref, t_ref, *, tile, width):
    x = x_ref[0]
    mod = mod_ref[0]
    sh, sc, gate = mod[0:1], mod[1:2], mod[2:3]
    h_ref[...] = (_rms(x, ng_ref[0:1]) * (1.0 + sc) + sh).astype(BF16)
    half = max(tile // GMLP_SPLIT, CHUNK)

    def dot_rows(lhs_ref, w):
        return jnp.concatenate([_dot(lhs_ref[r0:r0 + half], w) for r0 in range(0, tile, half)], axis=0)

    vsum = jnp.zeros((tile, 1), F32)
    for c0 in range(0, width, GMLP_CHUNK):
        vj = _gelu(dot_rows(h_ref, w_in_ref[:, width + c0:width + c0 + GMLP_CHUNK]))
        v_ref[:, c0:c0 + GMLP_CHUNK] = vj
        vsum = vsum + jnp.sum(vj, axis=-1, keepdims=True)
    mu = vsum * (1.0 / width)
    vss = jnp.zeros((tile, 1), F32)
    for c0 in range(0, width, GMLP_CHUNK):
        dv = v_ref[:, c0:c0 + GMLP_CHUNK] - mu
        vss = vss + jnp.sum(dv * dv, axis=-1, keepdims=True)
    rstd = lax.rsqrt(vss * (1.0 / width) + EPS)
    gw = width // A_GROUPS
    y = None
    for g in range(A_GROUPS):
        cols = slice(g * gw, (g + 1) * gw)
        vn = ((v_ref[:, cols] - mu) * rstd * lng_ref[:, cols] + lnb_ref[:, cols]).astype(BF16)
        u = _gelu(dot_rows(h_ref, w_in_ref[:, cols]))
        s = jnp.concatenate([_dot(ws_ref[g], vn[r0:r0 + CHUNK]) + bs_ref[g]
                             for r0 in range(0, tile, CHUNK)], axis=0)
        t_ref[:, cols] = (u * s).astype(BF16)
        if (g + 1) % GMLP_OUT_GROUPS == 0:
            k0 = (g + 1 - GMLP_OUT_GROUPS) * gw
            part = dot_rows(t_ref.at[:, k0:(g + 1) * gw], w_out_ref[k0:(g + 1) * gw, :])
            y = part if y is None else y + part
    o_ref[0] = x + gate * _rms(y, ng_ref[1:2])


def _gmlp(x, mod, ng, w_in, ln_g, ln_b, w_s, b_s, w_out, tile=512):
    bsz, n_tok, d = x.shape
    width = w_out.shape[0]
    gw = width // A_GROUPS
    tile = min(tile, n_tok)
    bs_full = jnp.broadcast_to(b_s[:, :, None], (A_GROUPS, CHUNK, gw))
    kern = functools.partial(_gmlp_kernel, tile=tile, width=width)
    return pl.pallas_call(
        kern,
        grid=(bsz, n_tok // tile),
        in_specs=[pl.BlockSpec((1, tile, d), lambda b, i: (b, i, 0)),
                  pl.BlockSpec((1, 6, d), lambda b, i: (b, 0, 0)),
                  _const_spec(ng.shape),
                  _const_spec(w_in.shape),
                  _const_spec((1, width)),
                  _const_spec((1, width)),
                  _const_spec(w_s.shape),
                  _const_spec(bs_full.shape),
                  _const_spec(w_out.shape)],
        out_specs=pl.BlockSpec((1, tile, d), lambda b, i: (b, i, 0)),
        out_shape=jax.ShapeDtypeStruct(x.shape, F32),
        scratch_shapes=[pltpu.VMEM((tile, d), BF16),
                        pltpu.VMEM((tile, width), F32),
                        pltpu.VMEM((tile, width), BF16)],
        compiler_params=_params(2),
        name="gmlp_mixer",
    )(x, mod, ng, w_in, ln_g.reshape(1, width), ln_b.reshape(1, width), w_s, bs_full, w_out)


def _ffn_kernel(x_ref, xp_ref, xn_ref, mod_ref, ng_ref, w_up_ref, cw_ref, cb_ref, w_dn_ref,
                o_ref, h_ref, *, tile, n_tiles, ffn):
    i = pl.program_id(1)
    mod = mod_ref[0]
    sh, sc, gate = mod[3:4], mod[4:5], mod[5:6]
    g_pre = ng_ref[2:3]

    def pre(xx):
        return _rms(xx, g_pre) * (1.0 + sc) + sh

    x = x_ref[0]
    has_prev = (i > 0).astype(F32)
    has_next = (i < n_tiles - 1).astype(F32)
    h_ref[0:HALO] = (pre(xp_ref[0]) * has_prev).astype(BF16)
    h_ref[HALO:HALO + tile] = pre(x).astype(BF16)
    h_ref[HALO + tile:] = (pre(xn_ref[0]) * has_next).astype(BF16)
    ext = tile + 2 * HALO
    y = None
    c0 = 0
    while c0 < ffn:
        cw = min(FFN_CHUNK, ffn - c0)
        cols = slice(c0, c0 + cw)
        cuts = [0] + [HALO + (tile * k) // FFN_SPLIT for k in range(1, FFN_SPLIT)] + [ext]
        inner = [HALO] + cuts[1:-1] + [HALO + tile]
        a_ext = jnp.concatenate([_dot(h_ref[r0:r1], w_up_ref[:, cols])
                                 for r0, r1 in zip(cuts[:-1], cuts[1:])], axis=0)
        a = (pltpu.roll(a_ext, 1, 0)[HALO:HALO + tile] * cw_ref[0:1, cols]
             + a_ext[HALO:HALO + tile] * cw_ref[1:2, cols]
             + pltpu.roll(a_ext, ext - 1, 0)[HALO:HALO + tile] * cw_ref[2:3, cols]
             + cb_ref[:, cols])
        w_b = w_up_ref[:, ffn + c0:ffn + c0 + cw]
        b = jnp.concatenate([_dot(h_ref[r0:r1], w_b) for r0, r1 in zip(inner[:-1], inner[1:])], axis=0)
        g = (_gelu(a) * b).astype(BF16)
        part = jnp.concatenate([_dot(g[r0 - HALO:r1 - HALO], w_dn_ref[cols, :])
                                for r0, r1 in zip(inner[:-1], inner[1:])], axis=0)
        y = part if y is None else y + part
        c0 += cw
    o_ref[0] = x + gate * _rms(y, ng_ref[3:4])


def _layer_spec(stacked_shape, layer):
    nd = len(stacked_shape) - 1
    return pl.BlockSpec((None,) + tuple(stacked_shape[1:]), lambda *_: (layer,) + (0,) * nd,
                        pipeline_mode=pl.Buffered(1))


def _ffn(x, mod, ng, w_up, conv_w, conv_b, w_down, layer, tile=512):
    bsz, n_tok, d = x.shape
    ffn = w_down.shape[1]
    tile = min(tile, n_tok)
    n_tiles = n_tok // tile
    hb = tile // HALO
    n_hb = n_tok // HALO
    kern = functools.partial(_ffn_kernel, tile=tile, n_tiles=n_tiles, ffn=ffn)
    return pl.pallas_call(
        kern,
        grid=(bsz, n_tiles),
        in_specs=[pl.BlockSpec((1, tile, d), lambda b, i: (b, i, 0)),
                  pl.BlockSpec((1, HALO, d), lambda b, i: (b, jnp.maximum(i * hb - 1, 0), 0)),
                  pl.BlockSpec((1, HALO, d), lambda b, i: (b, jnp.minimum((i + 1) * hb, n_hb - 1), 0)),
                  pl.BlockSpec((1, 6, d), lambda b, i: (b, 0, 0)),
                  _const_spec(ng.shape),
                  _layer_spec(w_up.shape, layer),
                  _const_spec(conv_w.shape),
                  _const_spec((1, ffn)),
                  _layer_spec(w_down.shape, layer)],
        out_specs=pl.BlockSpec((1, tile, d), lambda b, i: (b, i, 0)),
        out_shape=jax.ShapeDtypeStruct(x.shape, F32),
        scratch_shapes=[pltpu.VMEM((tile + 2 * HALO, d), BF16)],
        compiler_params=_params(2),
        name="conv_glu",
    )(x, x, x, mod, ng, w_up, conv_w, conv_b.reshape(1, ffn), w_down)


def _qkv_kernel(x_ref, mod_ref, ng_ref, w_ref, b_ref, cos_ref, sin_ref, q_ref, kt_ref, v_ref,
                *, q_dim, kv_dim):
    x = x_ref[0]
    mod = mod_ref[0]
    sh, sc = mod[0:1], mod[1:2]
    h = (_rms(x, ng_ref[0:1]) * (1.0 + sc) + sh).astype(BF16)
    half = h.shape[0] // 2
    qkv = jnp.concatenate([_dot(h[0:half], w_ref[...]), _dot(h[half:], w_ref[...])], axis=0) + b_ref[...]
    cosf = cos_ref[...]
    sinf = sin_ref[...]
    lane = lax.broadcasted_iota(jnp.int32, cosf.shape, 1)
    low_half = (lane % HEAD_DIM) < (HEAD_DIM // 2)
    scale = HEAD_DIM ** -0.5 * LOG2E
    for j in range((q_dim + kv_dim) // LANES):
        blk = qkv[:, j * LANES:(j + 1) * LANES]
        partner = jnp.where(low_half, pltpu.roll(blk, 96, 1), pltpu.roll(blk, 32, 1))
        r = blk * cosf + partner * sinf
        if j * LANES < q_dim:
            q_ref[0, :, j * LANES:(j + 1) * LANES] = (r * scale).astype(BF16)
        else:
            kt_ref[0, j * LANES - q_dim:(j + 1) * LANES - q_dim, :] = r.T.astype(BF16)
    v_ref[0] = qkv[:, q_dim + kv_dim:].astype(BF16)


def _qkv(x, mod, ng, w_qkv, b_qkv, cosf, sinf, tile=1024):
    bsz, n_tok, d = x.shape
    tile = min(tile, n_tok)
    q_dim = N_HEADS * HEAD_DIM
    kv_dim = N_KV_HEADS * HEAD_DIM
    n = q_dim + 2 * kv_dim
    kern = functools.partial(_qkv_kernel, q_dim=q_dim, kv_dim=kv_dim)
    return pl.pallas_call(
        kern,
        grid=(bsz, n_tok // tile),
        in_specs=[pl.BlockSpec((1, tile, d), lambda b, i: (b, i, 0)),
                  pl.BlockSpec((1, 6, d), lambda b, i: (b, 0, 0)),
                  _const_spec(ng.shape),
                  _const_spec(w_qkv.shape),
                  _const_spec((1, n)),
                  pl.BlockSpec((tile, LANES), lambda b, i: (i, 0)),
                  pl.BlockSpec((tile, LANES), lambda b, i: (i, 0))],
        out_specs=[pl.BlockSpec((1, tile, q_dim), lambda b, i: (b, i, 0)),
                   pl.BlockSpec((1, kv_dim, tile), lambda b, i: (b, 0, i)),
                   pl.BlockSpec((1, tile, kv_dim), lambda b, i: (b, i, 0))],
        out_shape=[jax.ShapeDtypeStruct((bsz, n_tok, q_dim), BF16),
                   jax.ShapeDtypeStruct((bsz, kv_dim, n_tok), BF16),
                   jax.ShapeDtypeStruct((bsz, n_tok, kv_dim), BF16)],
        compiler_params=_params(2),
        name="qkv_rope",
    )(x, mod, ng, w_qkv, b_qkv.reshape(1, n), cosf, sinf)


def _rope_tables(n_tok):
    rows = n_tok // GRID_W
    n_freq = HEAD_DIM // 4
    row = jnp.repeat(jnp.arange(rows), GRID_W).astype(F32)
    col = jnp.tile(jnp.arange(GRID_W), rows).astype(F32)
    inv_freq = ROPE_THETA ** (-(jnp.arange(n_freq, dtype=F32) / n_freq))
    ang = jnp.concatenate([row[:, None] * inv_freq, col[:, None] * inv_freq], axis=-1)
    cos, sin = jnp.cos(ang), jnp.sin(ang)
    cosf = jnp.tile(cos, (1, 4))
    sinf = jnp.tile(jnp.concatenate([-sin, sin], axis=-1), (1, 2))
    return cosf, sinf


def _attn_kernel(sink_ref, q_ref, ktp_ref, ktc_ref, ktn_ref, vp_ref, vc_ref, vn_ref, kctx_ref, vctx_ref,
                 x_ref, mod_ref, ng_ref, wo_ref, o_ref, kwin_ref, vwin_ref, attn_ref, bias_ref, s_ref, p_ref,
                 m_ref, ve_ref, vo_ref, vce_ref, vco_ref,
                 *, tq, n_tiles):
    i = pl.program_id(1)
    nb = tq // BLOCK
    kwin_ref[:, 0:BLOCK] = ktp_ref[0]
    kwin_ref[:, BLOCK:BLOCK + tq] = ktc_ref[0]
    kwin_ref[:, BLOCK + tq:] = ktn_ref[0]
    vwin_ref[0:BLOCK] = vp_ref[0]
    vwin_ref[BLOCK:BLOCK + tq] = vc_ref[0]
    vwin_ref[BLOCK + tq:] = vn_ref[0]
    band_w = 3 * BLOCK
    r = lax.broadcasted_iota(jnp.int32, (BLOCK, band_w), 0)
    c = lax.broadcasted_iota(jnp.int32, (BLOCK, band_w), 1)
    band = (c >= r) & (c <= r + 2 * BLOCK)
    first = band & ((c >= BLOCK) | (i > 0))
    last = band & ((c < 2 * BLOCK) | (i < n_tiles - 1))
    neg = jnp.float32(-jnp.inf)
    bias_ref[0, 0] = jnp.where(first, 0.0, neg)[:, 0:BLOCK]
    bias_ref[0, 1] = jnp.where(band, 0.0, neg)[:, 0:BLOCK]
    bias_ref[1, 0] = jnp.where(last, 0.0, neg)[:, 2 * BLOCK:]
    bias_ref[1, 1] = jnp.where(band, 0.0, neg)[:, 2 * BLOCK:]

    lane = lax.broadcasted_iota(jnp.int32, (1, 2 * HEAD_DIM), 1)
    low = lane < HEAD_DIM
    one = jnp.ones((), BF16)

    def pad_values(src_ref, even_ref, odd_ref, n_rows):
        for t in range(N_KV_HEADS // 2):
            src = src_ref[0:n_rows, t * LANES:(t + 1) * LANES]
            swapped = pltpu.roll(src, HEAD_DIM, 1)
            even_ref[0:n_rows, (2 * t) * LANES:(2 * t + 1) * LANES] = jnp.where(low, src, one)
            odd_ref[0:n_rows, (2 * t) * LANES:(2 * t + 1) * LANES] = jnp.where(low, one, swapped)
            even_ref[0:n_rows, (2 * t + 1) * LANES:(2 * t + 2) * LANES] = jnp.where(low, swapped, one)
            odd_ref[0:n_rows, (2 * t + 1) * LANES:(2 * t + 2) * LANES] = jnp.where(low, one, src)

    pad_values(vwin_ref, ve_ref, vo_ref, tq + 2 * BLOCK)
    pad_values(vctx_ref.at[0], vce_ref, vco_ref, vctx_ref.shape[1])

    n_rows = Q_PER_KV * BLOCK
    half = n_rows // 2
    row_order = (0, 2, 1, 3)
    unit = 0
    for b in range(nb):
        bp = 0 if b == 0 else 1
        bn = 0 if b == nb - 1 else 1
        q = q_ref[0, b * BLOCK:(b + 1) * BLOCK, :]
        for h in range(N_KV_HEADS):
            par = unit % 2
            unit += 1
            heads = [h * Q_PER_KV + g for g in row_order]
            feat = slice(h * HEAD_DIM, (h + 1) * HEAD_DIM)
            tile = slice(h * LANES, (h + 1) * LANES)
            keys = slice(b * BLOCK, b * BLOCK + band_w)
            qh = jnp.concatenate([q[:, n * HEAD_DIM:(n + 1) * HEAD_DIM] for n in heads], axis=0)
            s_ref[par, :, 0:band_w] = _dot(qh, kwin_ref[feat, keys])
            s_ref[par, :, band_w:] = _dot(qh, kctx_ref[0, feat, :])
            for ci in range(n_rows // SM_ROWS):
                rows = slice(ci * SM_ROWS, (ci + 1) * SM_ROWS)
                r0 = (ci * SM_ROWS) % BLOCK
                sink = sink_ref[heads[ci * SM_ROWS // BLOCK]] * LOG2E
                mt = jnp.maximum(s_ref[par, rows, 0:BLOCK] + bias_ref[0, bp, r0:r0 + SM_ROWS, :],
                                 s_ref[par, rows, 2 * BLOCK:band_w] + bias_ref[1, bn, r0:r0 + SM_ROWS, :])
                for t in (1, 3, 4):
                    mt = jnp.maximum(mt, s_ref[par, rows, t * BLOCK:(t + 1) * BLOCK])
                m = jnp.maximum(jnp.max(mt, axis=-1, keepdims=True), sink)
                m_ref[par, rows, :] = jnp.broadcast_to(m, (SM_ROWS, BLOCK))
            for ci in range(n_rows // SM_ROWS):
                rows = slice(ci * SM_ROWS, (ci + 1) * SM_ROWS)
                r0 = (ci * SM_ROWS) % BLOCK
                m = m_ref[par, rows, :]
                for t in range(5):
                    st = s_ref[par, rows, t * BLOCK:(t + 1) * BLOCK]
                    if t == 0:
                        st = st + bias_ref[0, bp, r0:r0 + SM_ROWS, :]
                    if t == 2:
                        st = st + bias_ref[1, bn, r0:r0 + SM_ROWS, :]
                    p_ref[par, rows, t * BLOCK:(t + 1) * BLOCK] = jnp.exp2(st - m).astype(BF16)
            for part, (v_ref, vc_ref) in enumerate(((ve_ref, vce_ref), (vo_ref, vco_ref))):
                rows = slice(part * half, (part + 1) * half)
                acc = (_dot(p_ref[par, rows, 0:band_w], v_ref[keys, tile])
                       + _dot(p_ref[par, rows, band_w:], vc_ref[:, tile]))
                e_sink = jnp.concatenate(
                    [jnp.exp2(sink_ref[heads[part * 2 + k]] * LOG2E
                              - m_ref[par, part * half + k * BLOCK:part * half + (k + 1) * BLOCK, :])
                     for k in range(2)], axis=0)
                denom = pltpu.roll(acc, HEAD_DIM, 1) + e_sink
                o_ref_half = acc / denom
                if part == 0:
                    o_even = o_ref_half
                else:
                    o_odd = o_ref_half
            for gp in range(Q_PER_KV // 2):
                pair = jnp.where(low, o_even[gp * BLOCK:(gp + 1) * BLOCK], o_odd[gp * BLOCK:(gp + 1) * BLOCK])
                col = (h * Q_PER_KV + 2 * gp) * HEAD_DIM
                attn_ref[b * BLOCK:(b + 1) * BLOCK, col:col + 2 * HEAD_DIM] = pair.astype(BF16)
    y = jnp.concatenate([_dot(attn_ref[0:tq // 2], wo_ref[...]), _dot(attn_ref[tq // 2:], wo_ref[...])], axis=0)
    gate = mod_ref[0][2:3]
    o_ref[0] = x_ref[0] + gate * _rms(y, ng_ref[1:2])


def _attn(x, q, kt, v, kctx_t, vctx, sink, mod, ng, w_o, tq=512):
    bsz, n_tok, d = x.shape
    n_tiles = n_tok // tq
    nb = tq // BLOCK
    n_blk = n_tok // BLOCK
    n_ctx = vctx.shape[1]
    q_dim = q.shape[2]
    kv_dim = v.shape[2]
    kern = functools.partial(_attn_kernel, tq=tq, n_tiles=n_tiles)
    cur_map = lambda b, i: (b, i, 0)
    batch_map = lambda b, i: (b, 0, 0)
    return pl.pallas_call(
        kern,
        grid=(bsz, n_tiles),
        in_specs=[pl.BlockSpec(memory_space=pltpu.SMEM),
                  pl.BlockSpec((1, tq, q_dim), cur_map),
                  pl.BlockSpec((1, kv_dim, BLOCK), lambda b, i: (b, 0, jnp.maximum(i * nb - 1, 0))),
                  pl.BlockSpec((1, kv_dim, tq), lambda b, i: (b, 0, i)),
                  pl.BlockSpec((1, kv_dim, BLOCK), lambda b, i: (b, 0, jnp.minimum((i + 1) * nb, n_blk - 1))),
                  pl.BlockSpec((1, BLOCK, kv_dim), lambda b, i: (b, jnp.maximum(i * nb - 1, 0), 0)),
                  pl.BlockSpec((1, tq, kv_dim), cur_map),
                  pl.BlockSpec((1, BLOCK, kv_dim), lambda b, i: (b, jnp.minimum((i + 1) * nb, n_blk - 1), 0)),
                  pl.BlockSpec((1, kv_dim, n_ctx), batch_map),
                  pl.BlockSpec((1, n_ctx, kv_dim), batch_map),
                  pl.BlockSpec((1, tq, d), cur_map),
                  pl.BlockSpec((1, 6, d), batch_map),
                  _const_spec(ng.shape),
                  _const_spec(w_o.shape)],
        out_specs=pl.BlockSpec((1, tq, d), cur_map),
        out_shape=jax.ShapeDtypeStruct(x.shape, F32),
        scratch_shapes=[pltpu.VMEM((kv_dim, tq + 2 * BLOCK), BF16),
                        pltpu.VMEM((tq + 2 * BLOCK, kv_dim), BF16),
                        pltpu.VMEM((tq, q_dim), BF16),
                        pltpu.VMEM((2, 2, BLOCK, BLOCK), F32),
                        pltpu.VMEM((2, Q_PER_KV * BLOCK, 3 * BLOCK + n_ctx), F32),
                        pltpu.VMEM((2, Q_PER_KV * BLOCK, 3 * BLOCK + n_ctx), BF16),
                        pltpu.VMEM((2, Q_PER_KV * BLOCK, BLOCK), F32),
                        pltpu.VMEM((tq + 2 * BLOCK, N_KV_HEADS * LANES), BF16),
                        pltpu.VMEM((tq + 2 * BLOCK, N_KV_HEADS * LANES), BF16),
                        pltpu.VMEM((n_ctx, N_KV_HEADS * LANES), BF16),
                        pltpu.VMEM((n_ctx, N_KV_HEADS * LANES), BF16)],
        compiler_params=_params(2),
        name="window_attn",
    )(sink, q, kt, kt, kt, v, v, v, kctx_t, vctx, x, mod, ng, w_o)


def kernel(x, c, ctx, c_ctx, ada_w, ada_b, norm_g, a_w_in, a_ln_g, a_ln_b, a_w_s, a_b_s, a_w_out,
           b_w_qkv, b_b_qkv, b_sink, b_w_o, f_w_up, f_conv_w, f_conv_b, f_w_down):
    bsz, n_tok, d = x.shape
    n_ctx = ctx.shape[1]
    depth = ada_w.shape[0]
    assert depth == 2, "layer 0 = gMLP mixer, layer 1 = windowed attention (last layer: no context update)"

    rows = 8 * ((bsz + 1 + 7) // 8)
    cond = jnp.zeros((rows, d), F32).at[:bsz].set(c).at[bsz].set(c_ctx)
    mods = _ada(cond, ada_w, ada_b)

    def lat_mod(i):
        return mods[i, :bsz].reshape(bsz, 6, d)

    def ctx_mod(i):
        return jnp.broadcast_to(mods[i, bsz].reshape(1, 6, d), (bsz, 6, d))

    def ctx_mod_flat(i):
        return mods[i, bsz].reshape(1, 6, d)

    bf = lambda w: w.astype(BF16)
    w_up, w_down = bf(f_w_up), bf(f_w_down)

    ctx_flat = lambda a: a.reshape(1, bsz * n_ctx, a.shape[-1])

    gm = (norm_g[0], bf(a_w_in[0]), a_ln_g[0], a_ln_b[0], bf(a_w_s[0]), a_b_s[0], bf(a_w_out[0]))
    ff0 = (norm_g[0], w_up, f_conv_w[0], f_conv_b[0], w_down, 0)
    x = _gmlp(x, lat_mod(0), *gm)
    ctx = _gmlp(ctx_flat(ctx), ctx_mod_flat(0), *gm).reshape(bsz, n_ctx, d)
    x = _ffn(x, lat_mod(0), *ff0)
    ctx = _ffn(ctx, ctx_mod(0), *ff0)

    cosf, sinf = _rope_tables(n_tok)
    w_qkv = bf(b_w_qkv[0])
    q, k, v = _qkv(x, lat_mod(1), norm_g[1], w_qkv, b_b_qkv[0], cosf, sinf)
    _, kctx, vctx = _qkv(ctx_flat(ctx), ctx_mod_flat(1), norm_g[1], w_qkv, b_b_qkv[0],
                         jnp.ones((bsz * n_ctx, LANES), F32), jnp.zeros((bsz * n_ctx, LANES), F32))
    kv_dim = kctx.shape[1]
    kctx = kctx.reshape(kv_dim, bsz, n_ctx).transpose(1, 0, 2)
    vctx = vctx.reshape(bsz, n_ctx, kv_dim)
    x = _attn(x, q, k, v, kctx, vctx, b_sink[0], lat_mod(1), norm_g[1], bf(b_w_o[0]))
    x = _ffn(x, lat_mod(1), norm_g[1], w_up, f_conv_w[1], f_conv_b[1], w_down, 1)
    return x
```

```python
import functools
import math

import jax
import jax.numpy as jnp
from jax import lax
from jax.experimental import pallas as pl
from jax.experimental.pallas import tpu as pltpu

EPS = 1e-6
GRID_W = 64
CHUNK = 128
A_GROUPS = 8
HEAD_DIM = 64
N_HEADS = 16
N_KV_HEADS = 4
Q_PER_KV = N_HEADS // N_KV_HEADS
BLOCK = 128
ROPE_THETA = 10000.0
LANES = 128
LOG2E = math.log2(math.e)
SM_ROWS = 32
GMLP_CHUNK = 512
GMLP_OUT_GROUPS = 8
GMLP_SPLIT = 2
FFN_CHUNK = 2816
FFN_SPLIT = 2
HALO = 16
VMEM_LIMIT = 56 * 1024 * 1024

F32 = jnp.float32
BF16 = jnp.bfloat16


def _rms(x, g):
    return x * lax.rsqrt(jnp.mean(x * x, axis=-1, keepdims=True) + EPS) * g


def _gelu(x):
    c = math.sqrt(2.0 / math.pi)
    return x * (0.5 * (1.0 + jnp.tanh(c * (x + 0.044715 * (x * x * x)))))


def _dot(a, b):
    return jnp.dot(a, b, preferred_element_type=F32)


def _params(n_grid, n_operands=None, fused=()):
    fusion = None if n_operands is None else [k in fused for k in range(n_operands)]
    return pltpu.CompilerParams(dimension_semantics=("parallel",) * n_grid,
                                vmem_limit_bytes=VMEM_LIMIT,
                                allow_input_fusion=fusion)


def _const_spec(shape):
    nd = len(shape)
    return pl.BlockSpec(shape, lambda *_: (0,) * nd, pipeline_mode=pl.Buffered(1))


def _ada_kernel(s_ref, w_ref, b_ref, o_ref):
    s = s_ref[...]
    s = s * jax.nn.sigmoid(s)
    o_ref[0] = _dot(s.astype(BF16), w_ref[0].astype(BF16)) + b_ref[0]


def _ada(cond, ada_w, ada_b, tn=1536):
    depth, d, n = ada_w.shape
    r = cond.shape[0]
    return pl.pallas_call(
        _ada_kernel,
        grid=(depth, n // tn),
        in_specs=[pl.BlockSpec((r, d), lambda i, j: (0, 0)),
                  pl.BlockSpec((1, d, tn), lambda i, j: (i, 0, j)),
                  pl.BlockSpec((1, 1, tn), lambda i, j: (i, 0, j))],
        out_specs=pl.BlockSpec((1, r, tn), lambda i, j: (i, 0, j)),
        out_shape=jax.ShapeDtypeStruct((depth, r, n), F32),
        compiler_params=_params(2),
        name="ada_mod",
    )(cond, ada_w, ada_b.reshape(depth, 1, n))


def _gmlp_kernel(x_ref, mod_ref, ng_ref, w_in_ref, lng_ref, lnb_ref, ws_ref, bs_ref, w_out_ref,
                 o_ref, h_ref, v_ref, t_ref, *, tile, width):
    x = x_ref[0]
    mod = mod_ref[0]
    sh, sc, gate = mod[0:1], mod[1:2], mod[2:3]
    h_ref[...] = (_rms(x, ng_ref[0:1]) * (1.0 + sc) + sh).astype(BF16)
    half = max(tile // GMLP_SPLIT, CHUNK)

    def dot_rows(lhs_ref, w):
        return jnp.concatenate([_dot(lhs_ref[r0:r0 + half], w) for r0 in range(0, tile, half)], axis=0)

    vsum = jnp.zeros((tile, 1), F32)
    for c0 in range(0, width, GMLP_CHUNK):
        vj = _gelu(dot_rows(h_ref, w_in_ref[:, width + c0:width + c0 + GMLP_CHUNK]))
        v_ref[:, c0:c0 + GMLP_CHUNK] = vj
        vsum = vsum + jnp.sum(vj, axis=-1, keepdims=True)
    mu = vsum * (1.0 / width)
    vss = jnp.zeros((tile, 1), F32)
    for c0 in range(0, width, GMLP_CHUNK):
        dv = v_ref[:, c0:c0 + GMLP_CHUNK] - mu
        vss = vss + jnp.sum(dv * dv, axis=-1, keepdims=True)
    rstd = lax.rsqrt(vss * (1.0 / width) + EPS)
    gw = width // A_GROUPS
    y = None
    for g in range(A_GROUPS):
        cols = slice(g * gw, (g + 1) * gw)
        vn = ((v_ref[:, cols] - mu) * rstd * lng_ref[:, cols] + lnb_ref[:, cols]).astype(BF16)
        u = _gelu(dot_rows(h_ref, w_in_ref[:, cols]))
        s = jnp.concatenate([_dot(ws_ref[g], vn[r0:r0 + CHUNK]) + bs_ref[g]
                             for r0 in range(0, tile, CHUNK)], axis=0)
        t_ref[:, cols] = (u * s).astype(BF16)
        if (g + 1) % GMLP_OUT_GROUPS == 0:
            k0 = (g + 1 - GMLP_OUT_GROUPS) * gw
            part = dot_rows(t_ref.at[:, k0:(g + 1) * gw], w_out_ref[k0:(g + 1) * gw, :])
            y = part if y is None else y + part
    o_ref[0] = x + gate * _rms(y, ng_ref[1:2])


def _gmlp(x, mod, ng, w_in, ln_g, ln_b, w_s, b_s, w_out, tile=512):
    bsz, n_tok, d = x.shape
    width = w_out.shape[0]
    gw = width // A_GROUPS
    tile = min(tile, n_tok)
    bs_full = jnp.broadcast_to(b_s[:, :, None], (A_GROUPS, CHUNK, gw))
    kern = functools.partial(_gmlp_kernel, tile=tile, width=width)
    return pl.pallas_call(
        kern,
        grid=(bsz, n_tok // tile),
        in_specs=[pl.BlockSpec((1, tile, d), lambda b, i: (b, i, 0)),
                  pl.BlockSpec((1, 6, d), lambda b, i: (b, 0, 0)),
                  _const_spec(ng.shape),
                  _const_spec(w_in.shape),
                  _const_spec((1, width)),
                  _const_spec((1, width)),
                  _const_spec(w_s.shape),
                  _const_spec(bs_full.shape),
                  _const_spec(w_out.shape)],
        out_specs=pl.BlockSpec((1, tile, d), lambda b, i: (b, i, 0)),
        out_shape=jax.ShapeDtypeStruct(x.shape, F32),
        scratch_shapes=[pltpu.VMEM((tile, d), BF16),
                        pltpu.VMEM((tile, width), F32),
                        pltpu.VMEM((tile, width), BF16)],
        compiler_params=_params(2, 9, fused=(3, 6, 7, 8)),
        name="gmlp_mixer",
    )(x, mod, ng, w_in, ln_g.reshape(1, width), ln_b.reshape(1, width), w_s, bs_full, w_out)


def _ffn_kernel(x_ref, xp_ref, xn_ref, mod_ref, ng_ref, w_up_ref, cw_ref, cb_ref, w_dn_ref,
                o_ref, h_ref, *, tile, n_tiles, ffn):
    i = pl.program_id(1)
    mod = mod_ref[0]
    sh, sc, gate = mod[3:4], mod[4:5], mod[5:6]
    g_pre = ng_ref[2:3]

    def pre(xx):
        return _rms(xx, g_pre) * (1.0 + sc) + sh

    x = x_ref[0]
    has_prev = (i > 0).astype(F32)
    has_next = (i < n_tiles - 1).astype(F32)
    h_ref[0:HALO] = (pre(xp_ref[0]) * has_prev).astype(BF16)
    h_ref[HALO:HALO + tile] = pre(x).astype(BF16)
    h_ref[HALO + tile:] = (pre(xn_ref[0]) * has_next).astype(BF16)
    ext = tile + 2 * HALO
    y = None
    c0 = 0
    while c0 < ffn:
        cw = min(FFN_CHUNK, ffn - c0)
        cols = slice(c0, c0 + cw)
        cuts = [0] + [HALO + (tile * k) // FFN_SPLIT for k in range(1, FFN_SPLIT)] + [ext]
        inner = [HALO] + cuts[1:-1] + [HALO + tile]
        a_ext = jnp.concatenate([_dot(h_ref[r0:r1], w_up_ref[:, cols])
                                 for r0, r1 in zip(cuts[:-1], cuts[1:])], axis=0)
        a = (pltpu.roll(a_ext, 1, 0)[HALO:HALO + tile] * cw_ref[0:1, cols]
             + a_ext[HALO:HALO + tile] * cw_ref[1:2, cols]
             + pltpu.roll(a_ext, ext - 1, 0)[HALO:HALO + tile] * cw_ref[2:3, cols]
             + cb_ref[:, cols])
        w_b = w_up_ref[:, ffn + c0:ffn + c0 + cw]
        b = jnp.concatenate([_dot(h_ref[r0:r1], w_b) for r0, r1 in zip(inner[:-1], inner[1:])], axis=0)
        g = (_gelu(a) * b).astype(BF16)
        part = jnp.concatenate([_dot(g[r0 - HALO:r1 - HALO], w_dn_ref[cols, :])
                                for r0, r1 in zip(inner[:-1], inner[1:])], axis=0)
        y = part if y is None else y + part
        c0 += cw
    o_ref[0] = x + gate * _rms(y, ng_ref[3:4])


def _layer_spec(stacked_shape, layer):
    nd = len(stacked_shape) - 1
    return pl.BlockSpec((None,) + tuple(stacked_shape[1:]), lambda *_: (layer,) + (0,) * nd,
                        pipeline_mode=pl.Buffered(1))


def _ffn(x, mod, ng, w_up, conv_w, conv_b, w_down, layer, tile=512):
    bsz, n_tok, d = x.shape
    ffn = w_down.shape[1]
    tile = min(tile, n_tok)
    n_tiles = n_tok // tile
    hb = tile // HALO
    n_hb = n_tok // HALO
    kern = functools.partial(_ffn_kernel, tile=tile, n_tiles=n_tiles, ffn=ffn)
    return pl.pallas_call(
        kern,
        grid=(bsz, n_tiles),
        in_specs=[pl.BlockSpec((1, tile, d), lambda b, i: (b, i, 0)),
                  pl.BlockSpec((1, HALO, d), lambda b, i: (b, jnp.maximum(i * hb - 1, 0), 0)),
                  pl.BlockSpec((1, HALO, d), lambda b, i: (b, jnp.minimum((i + 1) * hb, n_hb - 1), 0)),
                  pl.BlockSpec((1, 6, d), lambda b, i: (b, 0, 0)),
                  _const_spec(ng.shape),
                  _layer_spec(w_up.shape, layer),
                  _const_spec(conv_w.shape),
                  _const_spec((1, ffn)),
                  _layer_spec(w_down.shape, layer)],
        out_specs=pl.BlockSpec((1, tile, d), lambda b, i: (b, i, 0)),
        out_shape=jax.ShapeDtypeStruct(x.shape, F32),
        scratch_shapes=[pltpu.VMEM((tile + 2 * HALO, d), BF16)],
        compiler_params=_params(2, 9, fused=(5, 8)),
        name="conv_glu",
    )(x, x, x, mod, ng, w_up, conv_w, conv_b.reshape(1, ffn), w_down)


def _qkv_kernel(x_ref, mod_ref, ng_ref, w_ref, b_ref, cos_ref, sin_ref, q_ref, kt_ref, v_ref,
                *, q_dim, kv_dim):
    x = x_ref[0]
    mod = mod_ref[0]
    sh, sc = mod[0:1], mod[1:2]
    h = (_rms(x, ng_ref[0:1]) * (1.0 + sc) + sh).astype(BF16)
    half = h.shape[0] // 2
    qkv = jnp.concatenate([_dot(h[0:half], w_ref[...]), _dot(h[half:], w_ref[...])], axis=0) + b_ref[...]
    cosf = cos_ref[...]
    sinf = sin_ref[...]
    lane = lax.broadcasted_iota(jnp.int32, cosf.shape, 1)
    low_half = (lane % HEAD_DIM) < (HEAD_DIM // 2)
    scale = HEAD_DIM ** -0.5 * LOG2E
    for j in range((q_dim + kv_dim) // LANES):
        blk = qkv[:, j * LANES:(j + 1) * LANES]
        partner = jnp.where(low_half, pltpu.roll(blk, 96, 1), pltpu.roll(blk, 32, 1))
        r = blk * cosf + partner * sinf
        if j * LANES < q_dim:
            q_ref[0, :, j * LANES:(j + 1) * LANES] = (r * scale).astype(BF16)
        else:
            kt_ref[0, j * LANES - q_dim:(j + 1) * LANES - q_dim, :] = r.T.astype(BF16)
    v_ref[0] = qkv[:, q_dim + kv_dim:].astype(BF16)


def _qkv(x, mod, ng, w_qkv, b_qkv, cosf, sinf, tile=1024):
    bsz, n_tok, d = x.shape
    tile = min(tile, n_tok)
    q_dim = N_HEADS * HEAD_DIM
    kv_dim = N_KV_HEADS * HEAD_DIM
    n = q_dim + 2 * kv_dim
    kern = functools.partial(_qkv_kernel, q_dim=q_dim, kv_dim=kv_dim)
    return pl.pallas_call(
        kern,
        grid=(bsz, n_tok // tile),
        in_specs=[pl.BlockSpec((1, tile, d), lambda b, i: (b, i, 0)),
                  pl.BlockSpec((1, 6, d), lambda b, i: (b, 0, 0)),
                  _const_spec(ng.shape),
                  _const_spec(w_qkv.shape),
                  _const_spec((1, n)),
                  pl.BlockSpec((tile, LANES), lambda b, i: (i, 0)),
                  pl.BlockSpec((tile, LANES), lambda b, i: (i, 0))],
        out_specs=[pl.BlockSpec((1, tile, q_dim), lambda b, i: (b, i, 0)),
                   pl.BlockSpec((1, kv_dim, tile), lambda b, i: (b, 0, i)),
                   pl.BlockSpec((1, tile, kv_dim), lambda b, i: (b, i, 0))],
        out_shape=[jax.ShapeDtypeStruct((bsz, n_tok, q_dim), BF16),
                   jax.ShapeDtypeStruct((bsz, kv_dim, n_tok), BF16),
                   jax.ShapeDtypeStruct((bsz, n_tok, kv_dim), BF16)],
        compiler_params=_params(2, 7, fused=(3,)),
        name="qkv_rope",
    )(x, mod, ng, w_qkv, b_qkv.reshape(1, n), cosf, sinf)


def _rope_tables(n_tok):
    rows = n_tok // GRID_W
    n_freq = HEAD_DIM // 4
    row = jnp.repeat(jnp.arange(rows), GRID_W).astype(F32)
    col = jnp.tile(jnp.arange(GRID_W), rows).astype(F32)
    inv_freq = ROPE_THETA ** (-(jnp.arange(n_freq, dtype=F32) / n_freq))
    ang = jnp.concatenate([row[:, None] * inv_freq, col[:, None] * inv_freq], axis=-1)
    cos, sin = jnp.cos(ang), jnp.sin(ang)
    cosf = jnp.tile(cos, (1, 4))
    sinf = jnp.tile(jnp.concatenate([-sin, sin], axis=-1), (1, 2))
    return cosf, sinf


def _attn_kernel(sink_ref, q_ref, ktp_ref, ktc_ref, ktn_ref, vp_ref, vc_ref, vn_ref, kctx_ref, vctx_ref,
                 x_ref, mod_ref, ng_ref, wo_ref, o_ref, kwin_ref, vwin_ref, attn_ref, bias_ref, s_ref, p_ref,
                 m_ref, ve_ref, vo_ref, vce_ref, vco_ref,
                 *, tq, n_tiles):
    i = pl.program_id(1)
    nb = tq // BLOCK
    kwin_ref[:, 0:BLOCK] = ktp_ref[0]
    kwin_ref[:, BLOCK:BLOCK + tq] = ktc_ref[0]
    kwin_ref[:, BLOCK + tq:] = ktn_ref[0]
    vwin_ref[0:BLOCK] = vp_ref[0]
    vwin_ref[BLOCK:BLOCK + tq] = vc_ref[0]
    vwin_ref[BLOCK + tq:] = vn_ref[0]
    band_w = 3 * BLOCK
    r = lax.broadcasted_iota(jnp.int32, (BLOCK, band_w), 0)
    c = lax.broadcasted_iota(jnp.int32, (BLOCK, band_w), 1)
    band = (c >= r) & (c <= r + 2 * BLOCK)
    first = band & ((c >= BLOCK) | (i > 0))
    last = band & ((c < 2 * BLOCK) | (i < n_tiles - 1))
    neg = jnp.float32(-jnp.inf)
    bias_ref[0, 0] = jnp.where(first, 0.0, neg)[:, 0:BLOCK]
    bias_ref[0, 1] = jnp.where(band, 0.0, neg)[:, 0:BLOCK]
    bias_ref[1, 0] = jnp.where(last, 0.0, neg)[:, 2 * BLOCK:]
    bias_ref[1, 1] = jnp.where(band, 0.0, neg)[:, 2 * BLOCK:]

    lane = lax.broadcasted_iota(jnp.int32, (1, 2 * HEAD_DIM), 1)
    low = lane < HEAD_DIM
    one = jnp.ones((), BF16)

    def pad_values(src_ref, even_ref, odd_ref, n_rows):
        for t in range(N_KV_HEADS // 2):
            src = src_ref[0:n_rows, t * LANES:(t + 1) * LANES]
            swapped = pltpu.roll(src, HEAD_DIM, 1)
            even_ref[0:n_rows, (2 * t) * LANES:(2 * t + 1) * LANES] = jnp.where(low, src, one)
            odd_ref[0:n_rows, (2 * t) * LANES:(2 * t + 1) * LANES] = jnp.where(low, one, swapped)
            even_ref[0:n_rows, (2 * t + 1) * LANES:(2 * t + 2) * LANES] = jnp.where(low, swapped, one)
            odd_ref[0:n_rows, (2 * t + 1) * LANES:(2 * t + 2) * LANES] = jnp.where(low, one, src)

    pad_values(vwin_ref, ve_ref, vo_ref, tq + 2 * BLOCK)
    pad_values(vctx_ref.at[0], vce_ref, vco_ref, vctx_ref.shape[1])

    n_rows = Q_PER_KV * BLOCK
    half = n_rows // 2
    row_order = (0, 2, 1, 3)
    unit = 0
    for b in range(nb):
        bp = 0 if b == 0 else 1
        bn = 0 if b == nb - 1 else 1
        q = q_ref[0, b * BLOCK:(b + 1) * BLOCK, :]
        for h in range(N_KV_HEADS):
            par = unit % 2
            unit += 1
            heads = [h * Q_PER_KV + g for g in row_order]
            feat = slice(h * HEAD_DIM, (h + 1) * HEAD_DIM)
            tile = slice(h * LANES, (h + 1) * LANES)
            keys = slice(b * BLOCK, b * BLOCK + band_w)
            qh = jnp.concatenate([q[:, n * HEAD_DIM:(n + 1) * HEAD_DIM] for n in heads], axis=0)
            s_ref[par, :, 0:band_w] = _dot(qh, kwin_ref[feat, keys])
            s_ref[par, :, band_w:] = _dot(qh, kctx_ref[0, feat, :])
            for ci in range(n_rows // SM_ROWS):
                rows = slice(ci * SM_ROWS, (ci + 1) * SM_ROWS)
                r0 = (ci * SM_ROWS) % BLOCK
                sink = sink_ref[heads[ci * SM_ROWS // BLOCK]] * LOG2E
                mt = jnp.maximum(s_ref[par, rows, 0:BLOCK] + bias_ref[0, bp, r0:r0 + SM_ROWS, :],
                                 s_ref[par, rows, 2 * BLOCK:band_w] + bias_ref[1, bn, r0:r0 + SM_ROWS, :])
                for t in (1, 3, 4):
                    mt = jnp.maximum(mt, s_ref[par, rows, t * BLOCK:(t + 1) * BLOCK])
                m = jnp.maximum(jnp.max(mt, axis=-1, keepdims=True), sink)
                m_ref[par, rows, :] = jnp.broadcast_to(m, (SM_ROWS, BLOCK))
            for ci in range(n_rows // SM_ROWS):
                rows = slice(ci * SM_ROWS, (ci + 1) * SM_ROWS)
                r0 = (ci * SM_ROWS) % BLOCK
                m = m_ref[par, rows, :]
                for t in range(5):
                    st = s_ref[par, rows, t * BLOCK:(t + 1) * BLOCK]
                    if t == 0:
                        st = st + bias_ref[0, bp, r0:r0 + SM_ROWS, :]
                    if t == 2:
                        st = st + bias_ref[1, bn, r0:r0 + SM_ROWS, :]
                    p_ref[par, rows, t * BLOCK:(t + 1) * BLOCK] = jnp.exp2(st - m).astype(BF16)
            for part, (v_ref, vc_ref) in enumerate(((ve_ref, vce_ref), (vo_ref, vco_ref))):
                rows = slice(part * half, (part + 1) * half)
                acc = (_dot(p_ref[par, rows, 0:band_w], v_ref[keys, tile])
                       + _dot(p_ref[par, rows, band_w:], vc_ref[:, tile]))
                e_sink = jnp.concatenate(
                    [jnp.exp2(sink_ref[heads[part * 2 + k]] * LOG2E
                              - m_ref[par, part * half + k * BLOCK:part * half + (k + 1) * BLOCK, :])
                     for k in range(2)], axis=0)
                denom = pltpu.roll(acc, HEAD_DIM, 1) + e_sink
                o_ref_half = acc / denom
                if part == 0:
                    o_even = o_ref_half
                else:
                    o_odd = o_ref_half
            for gp in range(Q_PER_KV // 2):
                pair = jnp.where(low, o_even[gp * BLOCK:(gp + 1) * BLOCK], o_odd[gp * BLOCK:(gp + 1) * BLOCK])
                col = (h * Q_PER_KV + 2 * gp) * HEAD_DIM
                attn_ref[b * BLOCK:(b + 1) * BLOCK, col:col + 2 * HEAD_DIM] = pair.astype(BF16)
    y = jnp.concatenate([_dot(attn_ref[0:tq // 2], wo_ref[...]), _dot(attn_ref[tq // 2:], wo_ref[...])], axis=0)
    gate = mod_ref[0][2:3]
    o_ref[0] = x_ref[0] + gate * _rms(y, ng_ref[1:2])


def _attn(x, q, kt, v, kctx_t, vctx, sink, mod, ng, w_o, tq=512):
    bsz, n_tok, d = x.shape
    n_tiles = n_tok // tq
    nb = tq // BLOCK
    n_blk = n_tok // BLOCK
    n_ctx = vctx.shape[1]
    q_dim = q.shape[2]
    kv_dim = v.shape[2]
    kern = functools.partial(_attn_kernel, tq=tq, n_tiles=n_tiles)
    cur_map = lambda b, i: (b, i, 0)
    batch_map = lambda b, i: (b, 0, 0)
    return pl.pallas_call(
        kern,
        grid=(bsz, n_tiles),
        in_specs=[pl.BlockSpec(memory_space=pltpu.SMEM),
                  pl.BlockSpec((1, tq, q_dim), cur_map),
                  pl.BlockSpec((1, kv_dim, BLOCK), lambda b, i: (b, 0, jnp.maximum(i * nb - 1, 0))),
                  pl.BlockSpec((1, kv_dim, tq), lambda b, i: (b, 0, i)),
                  pl.BlockSpec((1, kv_dim, BLOCK), lambda b, i: (b, 0, jnp.minimum((i + 1) * nb, n_blk - 1))),
                  pl.BlockSpec((1, BLOCK, kv_dim), lambda b, i: (b, jnp.maximum(i * nb - 1, 0), 0)),
                  pl.BlockSpec((1, tq, kv_dim), cur_map),
                  pl.BlockSpec((1, BLOCK, kv_dim), lambda b, i: (b, jnp.minimum((i + 1) * nb, n_blk - 1), 0)),
                  pl.BlockSpec((1, kv_dim, n_ctx), batch_map),
                  pl.BlockSpec((1, n_ctx, kv_dim), batch_map),
                  pl.BlockSpec((1, tq, d), cur_map),
                  pl.BlockSpec((1, 6, d), batch_map),
                  _const_spec(ng.shape),
                  _const_spec(w_o.shape)],
        out_specs=pl.BlockSpec((1, tq, d), cur_map),
        out_shape=jax.ShapeDtypeStruct(x.shape, F32),
        scratch_shapes=[pltpu.VMEM((kv_dim, tq + 2 * BLOCK), BF16),
                        pltpu.VMEM((tq + 2 * BLOCK, kv_dim), BF16),
                        pltpu.VMEM((tq, q_dim), BF16),
                        pltpu.VMEM((2, 2, BLOCK, BLOCK), F32),
                        pltpu.VMEM((2, Q_PER_KV * BLOCK, 3 * BLOCK + n_ctx), F32),
                        pltpu.VMEM((2, Q_PER_KV * BLOCK, 3 * BLOCK + n_ctx), BF16),
                        pltpu.VMEM((2, Q_PER_KV * BLOCK, BLOCK), F32),
                        pltpu.VMEM((tq + 2 * BLOCK, N_KV_HEADS * LANES), BF16),
                        pltpu.VMEM((tq + 2 * BLOCK, N_KV_HEADS * LANES), BF16),
                        pltpu.VMEM((n_ctx, N_KV_HEADS * LANES), BF16),
                        pltpu.VMEM((n_ctx, N_KV_HEADS * LANES), BF16)],
        compiler_params=_params(2, 14, fused=(13,)),
        name="window_attn",
    )(sink, q, kt, kt, kt, v, v, v, kctx_t, vctx, x, mod, ng, w_o)


def kernel(x, c, ctx, c_ctx, ada_w, ada_b, norm_g, a_w_in, a_ln_g, a_ln_b, a_w_s, a_b_s, a_w_out,
           b_w_qkv, b_b_qkv, b_sink, b_w_o, f_w_up, f_conv_w, f_conv_b, f_w_down):
    bsz, n_tok, d = x.shape
    n_ctx = ctx.shape[1]
    depth = ada_w.shape[0]
    assert depth == 2, "layer 0 = gMLP mixer, layer 1 = windowed attention (last layer: no context update)"

    rows = 8 * ((bsz + 1 + 7) // 8)
    cond = jnp.zeros((rows, d), F32).at[:bsz].set(c).at[bsz].set(c_ctx)
    mods = _ada(cond, ada_w, ada_b)

    def lat_mod(i):
        return mods[i, :bsz].reshape(bsz, 6, d)

    def ctx_mod(i):
        return jnp.broadcast_to(mods[i, bsz].reshape(1, 6, d), (bsz, 6, d))

    def ctx_mod_flat(i):
        return mods[i, bsz].reshape(1, 6, d)

    bf = lambda w: w.astype(BF16)
    w_up, w_down = bf(f_w_up), bf(f_w_down)

    ctx_flat = lambda a: a.reshape(1, bsz * n_ctx, a.shape[-1])

    gm = (norm_g[0], bf(a_w_in[0]), a_ln_g[0], a_ln_b[0], bf(a_w_s[0]), a_b_s[0], bf(a_w_out[0]))
    ff0 = (norm_g[0], w_up, f_conv_w[0], f_conv_b[0], w_down, 0)
    x = _gmlp(x, lat_mod(0), *gm)
    ctx = _gmlp(ctx_flat(ctx), ctx_mod_flat(0), *gm).reshape(bsz, n_ctx, d)
    x = _ffn(x, lat_mod(0), *ff0)
    ctx = _ffn(ctx, ctx_mod(0), *ff0)

    cosf, sinf = _rope_tables(n_tok)
    w_qkv = bf(b_w_qkv[0])
    q, k, v = _qkv(x, lat_mod(1), norm_g[1], w_qkv, b_b_qkv[0], cosf, sinf)
    _, kctx, vctx = _qkv(ctx_flat(ctx), ctx_mod_flat(1), norm_g[1], w_qkv, b_b_qkv[0],
                         jnp.ones((bsz * n_ctx, LANES), F32), jnp.zeros((bsz * n_ctx, LANES), F32))
    kv_dim = kctx.shape[1]
    kctx = kctx.reshape(kv_dim, bsz, n_ctx).transpose(1, 0, 2)
    vctx = vctx.reshape(bsz, n_ctx, kv_dim)
    x = _attn(x, q, k, v, kctx, vctx, b_sink[0], lat_mod(1), norm_g[1], bf(b_w_o[0]))
    x = _ffn(x, lat_mod(1), norm_g[1], w_up, f_conv_w[1], f_conv_b[1], w_down, 1)
    return x
```
